```python
import math
import jax
import jax.numpy as jnp
from jax import lax
import numpy as np

D_MODEL = 2048
BATCH = 2
SEQ = 4096
DEPTH = 4
DEC_BATCH = 32
DEC_SEQ = 8
PAST_LEN = 16384
PAGE_SIZE = 128

MIX_WIDTH = D_MODEL
W_GROUP = MIX_WIDTH // 4
HGRN_WIDTH = W_GROUP
HGRN_HEAD_DIM = 128
HGRN_HEADS = HGRN_WIDTH // HGRN_HEAD_DIM
HGRN_CHUNK = 64
POOL_WIDTH = W_GROUP
POOL_WINDOWS = (2, 4, 8, 16)
POOL_GROUPS = 4
POOL_CH = POOL_WIDTH // POOL_GROUPS
POOL_BUF = 15
SSM_WIDTH = W_GROUP
SSM_HEAD_DIM = 64
SSM_HEADS = SSM_WIDTH // SSM_HEAD_DIM
SSM_STATE = 128
SSM_GROUPS = 2
SSM_CONV = 4
SSM_CONV_DIM = SSM_WIDTH + 2 * SSM_GROUPS * SSM_STATE
SSM_CHUNK = 64
ATTN_WIDTH = MIX_WIDTH - HGRN_WIDTH - POOL_WIDTH - SSM_WIDTH
ATTN_HEAD_DIM = 64
ATTN_HEADS = ATTN_WIDTH // ATTN_HEAD_DIM
ATTN_KV_HEADS = 2
WINDOW = 128
ROPE_THETA = 10000.0
D_FF = 4 * D_MODEL
N_MOD = 6
EPS = 1e-6
IN_SPLITS = (HGRN_WIDTH, HGRN_WIDTH, HGRN_WIDTH, HGRN_WIDTH, POOL_WIDTH, SSM_WIDTH, SSM_WIDTH, SSM_GROUPS * SSM_STATE, SSM_GROUPS * SSM_STATE, SSM_HEADS, ATTN_WIDTH, ATTN_KV_HEADS * ATTN_HEAD_DIM, ATTN_KV_HEADS * ATTN_HEAD_DIM)
IN_WIDTH = sum(IN_SPLITS)

kernel_name = "hymba_hgrn2_pool_ssd_swa_decoder_step"


def rmsnorm(x, g):
    xf = x.astype(jnp.float32)
    y = xf * lax.rsqrt(jnp.mean(xf * xf, axis=-1, keepdims=True) + EPS)
    return (y * g.astype(jnp.float32)).astype(x.dtype)


def to_chunks(a, L):
    b, t = a.shape[0], a.shape[1]
    return jnp.moveaxis(a.reshape((b, t // L, L) + a.shape[2:]), 1, 0)


def from_chunks(a):
    nc, b, L = a.shape[0], a.shape[1], a.shape[2]
    return jnp.moveaxis(a, 0, 1).reshape((b, nc * L) + a.shape[3:])


def causal_decay(cum, mask):
    diff = cum[:, :, None] - cum[:, None, :]
    m = mask.reshape(mask.shape + (1,) * (cum.ndim - 2))
    return jnp.exp(jnp.where(m[None], diff, -jnp.inf))


def hgrn_lower_bounds(lb_logits):
    p = jax.nn.softmax(lb_logits.astype(jnp.float32), axis=0)
    cs = jnp.cumsum(p, axis=0)
    return cs - cs[:1]


def rope(x, pos):
    half = x.shape[-1] // 2
    inv = ROPE_THETA ** (-jnp.arange(half, dtype=jnp.float32) / half)
    ang = pos.astype(jnp.float32)[:, None] * inv[None]
    cos = jnp.cos(ang)[None, :, None]
    sin = jnp.sin(ang)[None, :, None]
    x1, x2 = x[..., :half], x[..., half:]
    return jnp.concatenate([x1 * cos - x2 * sin, x2 * cos + x1 * sin], axis=-1)


def hgrn2_mixer(q_raw, f_raw, i_raw, g_raw, lb, norm_g, state):
    f32 = jnp.float32
    b, t, _ = q_raw.shape
    shp = (b, t, HGRN_HEADS, HGRN_HEAD_DIM)
    q = jax.nn.silu(q_raw.astype(f32)).reshape(shp)
    f = f_raw.astype(f32).reshape(shp)
    lbh = lb.astype(f32).reshape(HGRN_HEADS, HGRN_HEAD_DIM)
    log_f = jnp.logaddexp(jnp.log(lbh), jnp.log1p(-lbh) + jax.nn.log_sigmoid(f))
    k = (1.0 - lbh) * jax.nn.sigmoid(-f)
    v = i_raw.astype(f32).reshape(shp)
    L = math.gcd(t, HGRN_CHUNK)
    mask = jnp.tril(jnp.ones((L, L), dtype=bool))

    def step(S, blk):
        qc, kc, vc, lc = blk
        cum = jnp.cumsum(lc, axis=1)
        decay = causal_decay(cum, mask)
        scores = jnp.einsum('blhd,bshd,blshd->bhls', qc, kc, decay)
        o = jnp.einsum('bhls,bshv->blhv', scores, vc) + jnp.einsum('blhd,bhdv->blhv', qc * jnp.exp(cum), S)
        last = cum[:, -1]
        S = jnp.exp(last)[..., None] * S + jnp.einsum('bshd,bshv->bhdv', kc * jnp.exp(last[:, None] - cum), vc)
        return S, o

    S_fin, o = lax.scan(step, state.astype(f32), (to_chunks(q, L), to_chunks(k, L), to_chunks(v, L), to_chunks(log_f, L)))
    o = rmsnorm(from_chunks(o), norm_g.astype(f32).reshape(HGRN_HEADS, HGRN_HEAD_DIM))
    y = o.reshape(b, t, HGRN_WIDTH) * jax.nn.sigmoid(g_raw.astype(f32))
    return y.astype(q_raw.dtype), S_fin.astype(state.dtype)


def pool_mixer(u, buf, pos0, w_pool, scale):
    f32 = jnp.float32
    b, t, w_all = u.shape
    ext = jnp.concatenate([buf.astype(u.dtype), u], axis=1).astype(f32)
    cs = jnp.concatenate([jnp.zeros((b, 1, w_all), f32), jnp.cumsum(ext, axis=1)], axis=1)
    pos = pos0 + jnp.arange(t)
    outs = []
    for gi, win in enumerate(POOL_WINDOWS):
        sl = slice(gi * POOL_CH, (gi + 1) * POOL_CH)
        s = cs[:, POOL_BUF + 1:POOL_BUF + 1 + t, sl] - cs[:, POOL_BUF + 1 - win:POOL_BUF + 1 - win + t, sl]
        cnt = jnp.minimum(pos + 1, win).astype(f32)
        outs.append(s / cnt[None, :, None] - ext[:, POOL_BUF:, sl])
    pooled = jnp.stack(outs, axis=2)
    y = jnp.einsum('btgc,gcd->btgd', pooled, w_pool.astype(f32)).reshape(b, t, w_all) * scale.astype(f32)
    return y.astype(u.dtype), ext[:, -POOL_BUF:].astype(buf.dtype)


def ssd_mixer(z, xs, bs, cs_, dt_raw, conv_buf, ssm_state, conv_w, conv_b, dt_bias, a_log, d_skip, norm_g):
    f32 = jnp.float32
    b, t, _ = xs.shape
    G, R, P, N = SSM_GROUPS, SSM_HEADS // SSM_GROUPS, SSM_HEAD_DIM, SSM_STATE
    xbc = jnp.concatenate([xs, bs, cs_], axis=-1)
    ext = jnp.concatenate([conv_buf.astype(xbc.dtype), xbc], axis=1)
    acc = ext[:, 0:t].astype(f32) * conv_w[0].astype(f32)
    for j in range(1, SSM_CONV):
        acc = acc + ext[:, j:j + t].astype(f32) * conv_w[j].astype(f32)
    conv = jax.nn.silu(acc + conv_b.astype(f32))
    x, bm, cm = jnp.split(conv, [SSM_WIDTH, SSM_WIDTH + G * N], axis=-1)
    x = x.reshape(b, t, G, R, P)
    bm = bm.reshape(b, t, G, N)
    cm = cm.reshape(b, t, G, N)
    dt = jax.nn.softplus(dt_raw.astype(f32) + dt_bias.astype(f32)).reshape(b, t, G, R)
    a = dt * (-jnp.exp(a_log.astype(f32))).reshape(G, R)
    L = math.gcd(t, SSM_CHUNK)
    mask = jnp.tril(jnp.ones((L, L), dtype=bool))

    def step(S, blk):
        xc, bc, cc, dtc, ac = blk
        cum = jnp.cumsum(ac, axis=1)
        decay = causal_decay(cum, mask)
        cb = jnp.einsum('blgn,bsgn->blsg', cc, bc)
        wts = cb[..., None] * decay * dtc[:, None]
        y = jnp.einsum('blsgr,bsgrp->blgrp', wts, xc)
        y = y + jnp.einsum('blgn,bgrpn->blgrp', cc, S) * jnp.exp(cum)[..., None]
        last = cum[:, -1]
        w_in_state = dtc * jnp.exp(last[:, None] - cum)
        S = jnp.exp(last)[..., None, None] * S + jnp.einsum('bsgr,bsgrp,bsgn->bgrpn', w_in_state, xc, bc)
        return S, y

    S0 = ssm_state.astype(f32).reshape(b, G, R, P, N)
    S_fin, ys = lax.scan(step, S0, (to_chunks(x, L), to_chunks(bm, L), to_chunks(cm, L), to_chunks(dt, L), to_chunks(a, L)))
    y = from_chunks(ys) + x * d_skip.astype(f32).reshape(G, R)[..., None]
    y = y.reshape(b, t, SSM_WIDTH) * jax.nn.silu(z.astype(f32))
    y = rmsnorm(y, norm_g)
    return y.astype(xs.dtype), ext[:, -(SSM_CONV - 1):].astype(conv_buf.dtype), S_fin.reshape(b, SSM_HEADS, P, N).astype(ssm_state.dtype)


def swa_mixer(q_raw, k_raw, v_raw, k_buf, v_buf, pos0, q_norm_g, k_norm_g, sinks):
    f32 = jnp.float32
    b, t, _ = q_raw.shape
    H, KV, Dh = ATTN_HEADS, ATTN_KV_HEADS, ATTN_HEAD_DIM
    R = H // KV
    pos = pos0 + jnp.arange(t)
    q = rope(rmsnorm(q_raw.reshape(b, t, H, Dh), q_norm_g).astype(f32), pos)
    k = rope(rmsnorm(k_raw.reshape(b, t, KV, Dh), k_norm_g).astype(f32), pos).astype(k_buf.dtype)
    v = v_raw.reshape(b, t, KV, Dh).astype(v_buf.dtype)
    k_ext = jnp.concatenate([k_buf, k], axis=1)
    v_ext = jnp.concatenate([v_buf, v], axis=1)
    Lq = math.gcd(t, WINDOW)
    nb = t // Lq
    idx = jnp.arange(nb)[:, None] * Lq + jnp.arange(WINDOW + Lq)[None]
    kb = k_ext[:, idx].astype(f32)
    vb = v_ext[:, idx].astype(f32)
    qb = q.reshape(b, nb, Lq, KV, R, Dh)
    s = jnp.einsum('bnqgrd,bnkgd->bngrqk', qb, kb) * (Dh ** -0.5)
    q_pos = pos0 + jnp.arange(t).reshape(nb, Lq)
    k_pos = pos0 - WINDOW + idx
    kp, qp = k_pos[:, None, :], q_pos[:, :, None]
    mask = (kp >= 0) & (kp <= qp) & (kp > qp - WINDOW)
    s = jnp.where(mask[None, :, None, None], s, -jnp.inf)
    sink = sinks.astype(f32).reshape(1, 1, KV, R, 1, 1)
    m = jnp.maximum(jnp.max(s, axis=-1, keepdims=True), sink)
    p = jnp.exp(s - m)
    den = jnp.sum(p, axis=-1, keepdims=True) + jnp.exp(sink - m)
    o = jnp.einsum('bngrqk,bnkgd->bnqgrd', p / den, vb).reshape(b, t, ATTN_WIDTH)
    return o.astype(q_raw.dtype), k_ext[:, -WINDOW:], v_ext[:, -WINDOW:]


def trunk(x, c, pos0, s_hgrn, s_pool, s_ssm, s_conv, c_k, c_v,
          norm1_g, norm2_g, w_ada, b_ada, w_in, hgrn_lb_logits, hgrn_norm_g, pool_w, pool_scale,
          conv_w, conv_b, dt_bias, a_log, d_skip, ssm_norm_g, q_norm_g, k_norm_g, sinks, w_out, w_up, w_down):
    b = x.shape[0]
    lbs = hgrn_lower_bounds(hgrn_lb_logits)
    offsets = np.cumsum(np.array(IN_SPLITS))[:-1].tolist()
    new_hgrn, new_pool, new_ssm, new_conv, new_k, new_v = [], [], [], [], [], []
    for l in range(DEPTH):
        mod = (jax.nn.silu(c) @ w_ada[l] + b_ada[l]).reshape(b, N_MOD, 1, D_MODEL)
        shift1, scale1, gate1, shift2, scale2, gate2 = (mod[:, i] for i in range(N_MOD))
        h = rmsnorm(x, norm1_g[l]) * (1.0 + scale1) + shift1
        (aq, af, ai, ag, pu, cz, cx, cbm, ccm, cdt, dq, dk, dv) = jnp.split(h @ w_in[l], offsets, axis=-1)
        ya, sa = hgrn2_mixer(aq, af, ai, ag, lbs[l], hgrn_norm_g[l], s_hgrn[l])
        yb, sb = pool_mixer(pu, s_pool[l], pos0, pool_w[l], pool_scale[l])
        yc, sconv, sssm = ssd_mixer(cz, cx, cbm, ccm, cdt, s_conv[l], s_ssm[l], conv_w[l], conv_b[l], dt_bias[l], a_log[l], d_skip[l], ssm_norm_g[l])
        yd, sk, sv = swa_mixer(dq, dk, dv, c_k[l], c_v[l], pos0, q_norm_g[l], k_norm_g[l], sinks[l])
        mixed = jnp.concatenate([ya, yb, yc, yd], axis=-1) @ w_out[l]
        x = x + gate1 * mixed
        h = rmsnorm(x, norm2_g[l]) * (1.0 + scale2) + shift2
        x = x + gate2 * (jnp.square(jax.nn.relu(h @ w_up[l])) @ w_down[l])
        new_hgrn.append(sa)
        new_pool.append(sb)
        new_ssm.append(sssm)
        new_conv.append(sconv)
        new_k.append(sk)
        new_v.append(sv)
    return (x, jnp.stack(new_hgrn), jnp.stack(new_pool), jnp.stack(new_ssm), jnp.stack(new_conv), jnp.stack(new_k), jnp.stack(new_v))


def setup_inputs(seed: int = 0) -> dict:
    key = jax.random.key(seed)
    ks = jax.random.split(key, 40)
    f32 = jnp.float32

    def nrm(k, shape, s=1.0):
        return jax.random.normal(k, shape, f32) * s

    dt0 = jnp.exp(jax.random.uniform(ks[30], (DEPTH, SSM_HEADS), f32) * (math.log(0.1) - math.log(0.001)) + math.log(0.001))
    return {
        "x_prompt": nrm(ks[0], (BATCH, SEQ, D_MODEL)),
        "x_sample": nrm(ks[1], (DEC_BATCH, DEC_SEQ, D_MODEL)),
        "c_prompt": nrm(ks[2], (BATCH, D_MODEL)),
        "c_sample": nrm(ks[3], (DEC_BATCH, D_MODEL)),
        "state_hgrn": nrm(ks[4], (DEPTH, DEC_BATCH, HGRN_HEADS, HGRN_HEAD_DIM, HGRN_HEAD_DIM), 0.5),
        "state_pool": nrm(ks[5], (DEPTH, DEC_BATCH, POOL_BUF, POOL_WIDTH)),
        "state_ssm": nrm(ks[6], (DEPTH, DEC_BATCH, SSM_HEADS, SSM_HEAD_DIM, SSM_STATE), 0.1),
        "state_conv": nrm(ks[7], (DEPTH, DEC_BATCH, SSM_CONV - 1, SSM_CONV_DIM)),
        "cache_k": nrm(ks[8], (DEPTH, DEC_BATCH, WINDOW, ATTN_KV_HEADS, ATTN_HEAD_DIM)),
        "cache_v": nrm(ks[9], (DEPTH, DEC_BATCH, WINDOW, ATTN_KV_HEADS, ATTN_HEAD_DIM)),
        "norm1_g": 1.0 + nrm(ks[10], (DEPTH, D_MODEL), 0.02),
        "norm2_g": 1.0 + nrm(ks[11], (DEPTH, D_MODEL), 0.02),
        "w_ada": nrm(ks[12], (DEPTH, D_MODEL, N_MOD * D_MODEL), 0.5 * D_MODEL ** -0.5),
        "b_ada": nrm(ks[13], (DEPTH, N_MOD * D_MODEL), 0.02),
        "w_in": nrm(ks[14], (DEPTH, D_MODEL, IN_WIDTH), D_MODEL ** -0.5),
        "hgrn_lb_logits": nrm(ks[15], (DEPTH, HGRN_WIDTH), 0.5),
        "hgrn_norm_g": 1.0 + nrm(ks[16], (DEPTH, HGRN_WIDTH), 0.02),
        "pool_w": nrm(ks[17], (DEPTH, POOL_GROUPS, POOL_CH, POOL_CH), POOL_CH ** -0.5),
        "pool_scale": 1.0 + nrm(ks[18], (DEPTH, POOL_WIDTH), 0.02),
        "conv_w": nrm(ks[19], (DEPTH, SSM_CONV, SSM_CONV_DIM), SSM_CONV ** -0.5),
        "conv_b": nrm(ks[20], (DEPTH, SSM_CONV_DIM), 0.02),
        "dt_bias": dt0 + jnp.log(-jnp.expm1(-dt0)),
        "a_log": jnp.log(jax.random.uniform(ks[21], (DEPTH, SSM_HEADS), f32, 1.0, 16.0)),
        "d_skip": 1.0 + nrm(ks[22], (DEPTH, SSM_HEADS), 0.1),
        "ssm_norm_g": 1.0 + nrm(ks[23], (DEPTH, SSM_WIDTH), 0.02),
        "q_norm_g": 1.0 + nrm(ks[24], (DEPTH, ATTN_HEAD_DIM), 0.02),
        "k_norm_g": 1.0 + nrm(ks[25], (DEPTH, ATTN_HEAD_DIM), 0.02),
        "sinks": nrm(ks[26], (DEPTH, ATTN_HEADS)),
        "w_out": nrm(ks[27], (DEPTH, MIX_WIDTH, D_MODEL), MIX_WIDTH ** -0.5),
        "w_up": nrm(ks[28], (DEPTH, D_MODEL, D_FF), D_MODEL ** -0.5),
        "w_down": nrm(ks[29], (DEPTH, D_FF, D_MODEL), D_FF ** -0.5),
    }


def _zeros_like_state(a, b, dtype):
    return jnp.zeros((DEPTH, b) + a.shape[2:], dtype)


def reference(x_prompt, x_sample, c_prompt, c_sample, state_hgrn, state_pool, state_ssm, state_conv, cache_k, cache_v,
              norm1_g, norm2_g, w_ada, b_ada, w_in, hgrn_lb_logits, hgrn_norm_g, pool_w, pool_scale,
              conv_w, conv_b, dt_bias, a_log, d_skip, ssm_norm_g, q_norm_g, k_norm_g, sinks, w_out, w_up, w_down):
    weights = (norm1_g, norm2_g, w_ada, b_ada, w_in, hgrn_lb_logits, hgrn_norm_g, pool_w, pool_scale,
               conv_w, conv_b, dt_bias, a_log, d_skip, ssm_norm_g, q_norm_g, k_norm_g, sinks, w_out, w_up, w_down)
    bp = x_prompt.shape[0]
    dtp = x_prompt.dtype
    (y_prompt, p_hgrn, p_pool, p_ssm, p_conv, p_k, p_v) = trunk(
        x_prompt, c_prompt, 0,
        _zeros_like_state(state_hgrn, bp, dtp), _zeros_like_state(state_pool, bp, dtp),
        _zeros_like_state(state_ssm, bp, dtp), _zeros_like_state(state_conv, bp, dtp),
        _zeros_like_state(cache_k, bp, dtp), _zeros_like_state(cache_v, bp, dtp), *weights)
    (y_sample, s_hgrn, s_pool, s_ssm, s_conv, s_k, s_v) = trunk(
        x_sample, c_sample, PAST_LEN, state_hgrn, state_pool, state_ssm, state_conv, cache_k, cache_v, *weights)
    return (y_prompt, y_sample, p_hgrn, p_pool, p_ssm, p_conv, p_k, p_v, s_hgrn, s_pool, s_ssm, s_conv, s_k, s_v)
```

```python
import functools
import math

import jax
import jax.numpy as jnp
from jax import lax
from jax.experimental import pallas as pl
from jax.experimental.pallas import tpu as pltpu

F32 = jnp.float32
BF16 = jnp.bfloat16

D_MODEL = 2048
DEPTH = 4
PAST_LEN = 16384
W_GROUP = 512
HGRN_HEADS = 4
HGRN_DH = 128
POOL_WINDOWS = (2, 4, 8, 16)
POOL_CH = 128
POOL_BUF = 15
SSM_HEADS = 8
SSM_P = 64
SSM_N = 128
SSM_GROUPS = 2
SSM_CONV = 4
SSM_CONV_DIM = 1024
ATTN_HEADS = 8
ATTN_KV = 2
ATTN_DH = 64
WINDOW = 128
ROPE_THETA = 10000.0
D_FF = 4 * D_MODEL
N_MOD = 6
EPS = 1e-6

LANES = 128
SUBLANES = 8
TILE_ROWS = 128
HIST_PAD = 16
VMEM_LIMIT = 56 * 1024 * 1024

COL_AQ, COL_AF, COL_AI, COL_AG = 0, 512, 1024, 1536
COL_PU, COL_CZ, COL_XBC = 2048, 2560, 3072
COL_DQ, COL_DK, COL_DV, COL_DT = 4096, 4608, 4736, 4864
IN_WIDTH_P = 4992
NEG_BIG = -1e30

_NT = (((1,), (1,)), ((), ()))
_TN = (((0,), (0,)), ((), ()))


def _dot(a, b):
    return jnp.dot(a.astype(BF16), b.astype(BF16), preferred_element_type=F32)


def _dot_nt(a, b):
    return lax.dot_general(a.astype(BF16), b.astype(BF16), _NT, preferred_element_type=F32)


def _dot_tn(a, b):
    return lax.dot_general(a.astype(BF16), b.astype(BF16), _TN, preferred_element_type=F32)


def _sigmoid(x):
    return 1.0 / (1.0 + jnp.exp(-x))


def _cparams(sem):
    return pltpu.CompilerParams(dimension_semantics=sem, vmem_limit_bytes=VMEM_LIMIT)


def _ada_kernel(c_ref, w_ref, b_ref, o_ref):
    c = c_ref[...]
    s = c * _sigmoid(c)
    o_ref[0] = _dot(s, w_ref[0]) + b_ref[0]


def _ada_call(c_all, w_ada, b_ada):
    rows = c_all.shape[0]
    n = w_ada.shape[-1]
    tn = 1024
    return pl.pallas_call(
        _ada_kernel,
        grid=(DEPTH, n // tn),
        in_specs=[
            pl.BlockSpec((rows, D_MODEL), lambda l, j: (0, 0)),
            pl.BlockSpec((1, D_MODEL, tn), lambda l, j: (l, 0, j)),
            pl.BlockSpec((1, 1, tn), lambda l, j: (l, 0, j)),
        ],
        out_specs=pl.BlockSpec((1, rows, tn), lambda l, j: (l, 0, j)),
        out_shape=jax.ShapeDtypeStruct((DEPTH, rows, n), F32),
        compiler_params=_cparams(("arbitrary", "arbitrary")),
        name="ada_mod",
    )(c_all, w_ada, b_ada.reshape(DEPTH, 1, n))


def _modulated_norm(x, g, scale, shift):
    ms = jnp.mean(x * x, axis=-1, keepdims=True)
    y = x * lax.rsqrt(ms + EPS) * g
    return y * (1.0 + scale) + shift


def _inproj_kernel(x_ref, g_ref, sc_ref, sh_ref, w_ref, o_ref, h_ref):
    @pl.when(pl.program_id(1) == 0)
    def _():
        h_ref[...] = _modulated_norm(x_ref[...], g_ref[...], sc_ref[...], sh_ref[...]).astype(BF16)

    o_ref[...] = jnp.dot(h_ref[...], w_ref[...], preferred_element_type=F32)


def _inproj_call(grp, x, g, mod, w):
    m, tm = grp["m"], grp["tm"]
    tn = IN_WIDTH_P // 3
    return pl.pallas_call(
        _inproj_kernel,
        grid=(m // tm, IN_WIDTH_P // tn),
        in_specs=[
            pl.BlockSpec((tm, D_MODEL), lambda i, j: (i, 0)),
            pl.BlockSpec((1, D_MODEL), lambda i, j: (0, 0)),
            grp["mod_spec"](1),
            grp["mod_spec"](0),
            pl.BlockSpec((D_MODEL, tn), lambda i, j: (0, j)),
        ],
        out_specs=pl.BlockSpec((tm, tn), lambda i, j: (i, j)),
        out_shape=jax.ShapeDtypeStruct((m, IN_WIDTH_P), F32),
        scratch_shapes=[pltpu.VMEM((tm, D_MODEL), BF16)],
        compiler_params=_cparams(("arbitrary", "arbitrary")),
        name="in_proj",
    )(x, g.reshape(1, D_MODEL), mod, mod, w)


def _outproj_kernel(x_ref, ya_ref, yb_ref, yc_ref, yd_ref, gate_ref, w_ref, o_ref):
    acc = _dot(ya_ref[...], w_ref[0:W_GROUP, :])
    acc += _dot(yb_ref[...], w_ref[W_GROUP:2 * W_GROUP, :])
    acc += _dot(yc_ref[...], w_ref[2 * W_GROUP:3 * W_GROUP, :])
    acc += _dot(yd_ref[...], w_ref[3 * W_GROUP:4 * W_GROUP, :])
    o_ref[...] = x_ref[...] + gate_ref[...] * acc


def _outproj_call(grp, x, ys, mod, w):
    m, tm = grp["m"], grp["tm_out"]
    yspec = pl.BlockSpec((tm, W_GROUP), lambda i: (i, 0))
    return pl.pallas_call(
        _outproj_kernel,
        grid=(m // tm,),
        in_specs=[
            pl.BlockSpec((tm, D_MODEL), lambda i: (i, 0)),
            yspec, yspec, yspec, yspec,
            grp["mod_spec"](2, tm),
            pl.BlockSpec((D_MODEL, D_MODEL), lambda i: (0, 0)),
        ],
        out_specs=pl.BlockSpec((tm, D_MODEL), lambda i: (i, 0)),
        out_shape=jax.ShapeDtypeStruct((m, D_MODEL), F32),
        compiler_params=_cparams(("arbitrary",)),
        name="out_proj",
    )(x, *ys, mod, w)


def _mlp_kernel(x_ref, g_ref, sc_ref, sh_ref, gate_ref, wu_ref, wd_ref, o_ref, h_ref, acc_ref):
    j = pl.program_id(1)

    @pl.when(j == 0)
    def _():
        h_ref[...] = _modulated_norm(x_ref[...], g_ref[...], sc_ref[...], sh_ref[...]).astype(BF16)
        acc_ref[...] = jnp.zeros_like(acc_ref)

    u = jnp.dot(h_ref[...], wu_ref[...], preferred_element_type=F32)
    a = jnp.square(jnp.maximum(u, 0.0))
    acc_ref[...] += jnp.dot(a.astype(BF16), wd_ref[...], preferred_element_type=F32)

    @pl.when(j == pl.num_programs(1) - 1)
    def _():
        o_ref[...] = x_ref[...] + gate_ref[...] * acc_ref[...]


def _mlp_call(grp, x, g, mod, wu, wd):
    m, tm = grp["m"], grp["tm"]
    tf = 1024
    return pl.pallas_call(
        _mlp_kernel,
        grid=(m // tm, D_FF // tf),
        in_specs=[
            pl.BlockSpec((tm, D_MODEL), lambda i, j: (i, 0)),
            pl.BlockSpec((1, D_MODEL), lambda i, j: (0, 0)),
            grp["mod_spec"](4),
            grp["mod_spec"](3),
            grp["mod_spec"](5),
            pl.BlockSpec((D_MODEL, tf), lambda i, j: (0, j)),
            pl.BlockSpec((tf, D_MODEL), lambda i, j: (j, 0)),
        ],
        out_specs=pl.BlockSpec((tm, D_MODEL), lambda i, j: (i, 0)),
        out_shape=jax.ShapeDtypeStruct((m, D_MODEL), F32),
        scratch_shapes=[pltpu.VMEM((tm, D_MODEL), BF16), pltpu.VMEM((tm, D_MODEL), F32)],
        compiler_params=_cparams(("arbitrary", "arbitrary")),
        name="mlp",
    )(x, g.reshape(1, D_MODEL), mod, mod, mod, wu, wd)


def _row_iota(shape):
    return lax.broadcasted_iota(jnp.int32, shape, 0)


def _lane_iota(shape):
    return lax.broadcasted_iota(jnp.int32, shape, 1)


def _cumsum_rows(x, lseq):
    pos = _row_iota(x.shape) & (lseq - 1)
    s = 1
    while s < lseq:
        x = x + jnp.where(pos >= s, pltpu.roll(x, s, axis=0), 0.0)
        s *= 2
    return x


def _seq_last_rows(x, nseq, lseq):
    pieces = [jnp.broadcast_to(x[(b + 1) * lseq - 1:(b + 1) * lseq, :], (lseq, x.shape[1])) for b in range(nseq)]
    return pieces[0] if nseq == 1 else jnp.concatenate(pieces, axis=0)


def _mixer_spec(width, col, nt):
    return pl.BlockSpec((TILE_ROWS, width), lambda o, i: (o * nt + i, col // width))


def _state_spec(shape_tail, nseq):
    zeros = (0,) * len(shape_tail)
    return pl.BlockSpec((nseq,) + tuple(shape_tail), lambda o, i: (o,) + zeros)


def _const_spec(shape):
    zeros = (0,) * len(shape)
    return pl.BlockSpec(tuple(shape), lambda o, i: zeros)


def _hgrn_kernel(*refs, nseq, lseq, has_state):
    if has_state:
        aq_ref, af_ref, ai_ref, ag_ref, par_ref, s0_ref, y_ref, st_ref, kpad, cpad, vpad = refs
    else:
        aq_ref, af_ref, ai_ref, ag_ref, par_ref, y_ref, st_ref, kpad, cpad, vpad = refs
        s0_ref = None
    rows = TILE_ROWS
    dh = HGRN_DH

    @pl.when(pl.program_id(1) == 0)
    def _():
        if has_state:
            st_ref[...] = s0_ref[...]
        else:
            st_ref[...] = jnp.zeros_like(st_ref)

    zpad = jnp.zeros((HIST_PAD, W_GROUP), F32)
    kpad[0:HIST_PAD, :] = zpad
    cpad[0:HIST_PAD, :] = zpad
    vpad[0:HIST_PAD, :] = zpad

    aq = aq_ref[...]
    xf = af_ref[...]
    v = ai_ref[...]
    q = aq * _sigmoid(aq)
    e = jnp.exp(-jnp.abs(xf))
    inv = 1.0 / (1.0 + e)
    log_sig = jnp.minimum(xf, 0.0) - jnp.log1p(e)
    log_lb = par_ref[0:1, :]
    bterm = par_ref[1:2, :] + log_sig
    lf = jnp.maximum(log_lb, bterm) + jnp.log1p(jnp.exp(-jnp.abs(log_lb - bterm)))
    k = par_ref[2:3, :] * jnp.where(xf >= 0.0, e * inv, inv)

    cum = _cumsum_rows(lf, lseq)
    kpad[HIST_PAD:, :] = k
    cpad[HIST_PAD:, :] = cum
    vpad[HIST_PAD:, :] = v

    row = _row_iota((rows, W_GROUP))
    o = [jnp.zeros((rows, dh), F32) for _ in range(HGRN_HEADS)]

    sub = min(16, lseq)
    pos_sub = row & (sub - 1)
    for d in range(sub):
        if d == 0:
            term = q * k
            vs = v
        else:
            ks = kpad[pl.ds(HIST_PAD - d, rows), :]
            cs = cpad[pl.ds(HIST_PAD - d, rows), :]
            vs = vpad[pl.ds(HIST_PAD - d, rows), :]
            term = q * ks * jnp.exp(jnp.where(pos_sub >= d, cum - cs, NEG_BIG))
        for hd in range(HGRN_HEADS):
            sl = slice(hd * dh, (hd + 1) * dh)
            sc = jnp.sum(term[:, sl], axis=-1, keepdims=True)
            o[hd] = o[hd] + sc * vs[:, sl]

    levels = []
    h = lseq // 2
    while h >= sub:
        levels.append(h)
        h //= 2
    if levels:
        rr = _row_iota((rows, rows))
        cc = _lane_iota((rows, rows))
        p = [jnp.zeros((rows, rows), F32) for _ in range(HGRN_HEADS)]
        for h in levels:
            upper = (row & h) != 0
            pieces = []
            for jb in range(rows // (2 * h)):
                mid = jb * 2 * h + h
                pieces.append(jnp.broadcast_to(cpad[pl.ds(HIST_PAD + mid - 1, 1), :], (2 * h, W_GROUP)))
            refm = pieces[0] if len(pieces) == 1 else jnp.concatenate(pieces, axis=0)
            x = jnp.exp(jnp.where(upper, cum - refm, refm - cum))
            a_side = jnp.where(upper, q * x, 0.0).astype(BF16)
            b_side = jnp.where(upper, 0.0, k * x).astype(BF16)
            if 2 * h < rows:
                same = (rr // (2 * h)) == (cc // (2 * h))
            for hd in range(HGRN_HEADS):
                sl = slice(hd * dh, (hd + 1) * dh)
                s = lax.dot_general(a_side[:, sl], b_side[:, sl], _NT, preferred_element_type=F32)
                if 2 * h < rows:
                    s = jnp.where(same, s, 0.0)
                p[hd] = p[hd] + s
        for hd in range(HGRN_HEADS):
            sl = slice(hd * dh, (hd + 1) * dh)
            o[hd] = o[hd] + _dot(p[hd], v[:, sl])

    qe = (q * jnp.exp(cum)).astype(BF16)
    lastm = _seq_last_rows(cum, nseq, lseq)
    kd = k * jnp.exp(lastm - cum)
    dec = jnp.exp(lastm)
    vb = v.astype(BF16)
    seq_of_row = row // lseq
    for hd in range(HGRN_HEADS):
        sl = slice(hd * dh, (hd + 1) * dh)
        inter = []
        for b in range(nseq):
            rs = slice(b * lseq, (b + 1) * lseq)
            st = st_ref[b, hd]
            inter.append(lax.dot_general(qe[rs, sl], st.astype(BF16), _NT, preferred_element_type=F32))
            kdb = kd if nseq == 1 else jnp.where(seq_of_row == b, kd, 0.0)
            upd = lax.dot_general(vb[:, sl], kdb[:, sl].astype(BF16), _TN, preferred_element_type=F32)
            st_ref[b, hd] = st * dec[b * lseq:b * lseq + 1, sl] + upd
        o[hd] = o[hd] + (inter[0] if nseq == 1 else jnp.concatenate(inter, axis=0))

    ag = ag_ref[...]
    gate = _sigmoid(ag)
    for hd in range(HGRN_HEADS):
        sl = slice(hd * dh, (hd + 1) * dh)
        ms = jnp.mean(o[hd] * o[hd], axis=-1, keepdims=True)
        y_ref[:, sl] = o[hd] * lax.rsqrt(ms + EPS) * par_ref[3:4, sl] * gate[:, sl]


def _hgrn_call(grp, proj, par, s0):
    nseq, lseq, no, nt = grp["nseq"], grp["lseq"], grp["no"], grp["nt"]
    has_state = s0 is not None
    st_tail = (HGRN_HEADS, HGRN_DH, HGRN_DH)
    in_specs = [_mixer_spec(W_GROUP, c, nt) for c in (COL_AQ, COL_AF, COL_AI, COL_AG)] + [_const_spec((8, W_GROUP))]
    args = [proj, proj, proj, proj, par]
    if has_state:
        in_specs.append(_state_spec(st_tail, nseq))
        args.append(s0)
    pad = pltpu.VMEM((HIST_PAD + TILE_ROWS, W_GROUP), F32)
    return pl.pallas_call(
        functools.partial(_hgrn_kernel, nseq=nseq, lseq=lseq, has_state=has_state),
        grid=(no, nt),
        in_specs=in_specs,
        out_specs=[_mixer_spec(W_GROUP, 0, nt), _state_spec(st_tail, nseq)],
        out_shape=[jax.ShapeDtypeStruct((grp["m"], W_GROUP), F32),
                   jax.ShapeDtypeStruct((grp["nb"],) + st_tail, F32)],
        scratch_shapes=[pad, pad, pad],
        compiler_params=_cparams(("arbitrary", "arbitrary")),
        name="hgrn",
    )(*args)


def _pool_kernel(*refs, nseq, lseq, has_state, pos0):
    if has_state:
        u_ref, pw_ref, sc_ref, h0_ref, y_ref, hout_ref, ext = refs
    else:
        u_ref, pw_ref, sc_ref, y_ref, hout_ref, ext = refs
        h0_ref = None
    i = pl.program_id(1)
    stride = lseq + HIST_PAD

    @pl.when(i == 0)
    def _():
        if has_state:
            hout_ref[...] = h0_ref[...]
        else:
            hout_ref[...] = jnp.zeros_like(hout_ref)

    u = u_ref[...]
    for b in range(nseq):
        base = b * stride
        ext[base + HIST_PAD - POOL_BUF:base + HIST_PAD, :] = hout_ref[b]
        ext[base + HIST_PAD:base + HIST_PAD + lseq, :] = u[b * lseq:(b + 1) * lseq, :]

    local = _row_iota((TILE_ROWS, POOL_CH)) & (lseq - 1)
    posn = pos0 + i * lseq + local
    for gi, win in enumerate(POOL_WINDOWS):
        cs = slice(gi * POOL_CH, (gi + 1) * POOL_CH)
        pieces = []
        for b in range(nseq):
            base = b * stride + HIST_PAD
            s = ext[base:base + lseq, cs]
            for j in range(1, win):
                s = s + ext[pl.ds(base - j, lseq), cs]
            pieces.append(s)
        s = pieces[0] if nseq == 1 else jnp.concatenate(pieces, axis=0)
        cnt = jnp.minimum(posn + 1, win).astype(F32)
        pooled = s / cnt - u[:, cs]
        y_ref[:, cs] = _dot(pooled, pw_ref[gi]) * sc_ref[:, cs]

    for b in range(nseq):
        base = b * stride
        hout_ref[b] = ext[base + lseq + HIST_PAD - POOL_BUF:base + lseq + HIST_PAD, :]


def _pool_call(grp, proj, pw, scale, h0):
    nseq, lseq, no, nt = grp["nseq"], grp["lseq"], grp["no"], grp["nt"]
    has_state = h0 is not None
    tail = (POOL_BUF, W_GROUP)
    in_specs = [_mixer_spec(W_GROUP, COL_PU, nt), _const_spec((4, POOL_CH, POOL_CH)), _const_spec((1, W_GROUP))]
    args = [proj, pw, scale.reshape(1, W_GROUP)]
    if has_state:
        in_specs.append(_state_spec(tail, nseq))
        args.append(h0)
    return pl.pallas_call(
        functools.partial(_pool_kernel, nseq=nseq, lseq=lseq, has_state=has_state, pos0=grp["pos0"]),
        grid=(no, nt),
        in_specs=in_specs,
        out_specs=[_mixer_spec(W_GROUP, 0, nt), _state_spec(tail, nseq)],
        out_shape=[jax.ShapeDtypeStruct((grp["m"], W_GROUP), F32),
                   jax.ShapeDtypeStruct((grp["nb"],) + tail, F32)],
        scratch_shapes=[pltpu.VMEM((nseq * (lseq + HIST_PAD), W_GROUP), F32)],
        compiler_params=_cparams(("arbitrary", "arbitrary")),
        name="pool",
    )(*args)


def _expand_heads(z, emat):
    hi = z.astype(BF16)
    lo = (z - hi.astype(F32)).astype(BF16)
    return (jnp.dot(hi, emat, preferred_element_type=F32) + jnp.dot(lo, emat, preferred_element_type=F32))


def _ssd_kernel(*refs, nseq, lseq, has_state):
    if has_state:
        (z_ref, xbc_ref, dt_ref, cw_ref, cb_ref, par_ref, par5_ref, c0_ref, s0_ref,
         y_ref, cout_ref, st_ref, ext) = refs
    else:
        (z_ref, xbc_ref, dt_ref, cw_ref, cb_ref, par_ref, par5_ref,
         y_ref, cout_ref, st_ref, ext) = refs
        c0_ref = s0_ref = None
    rows = TILE_ROWS
    hist = SSM_CONV - 1
    stride = lseq + SUBLANES
    gw = SSM_N

    @pl.when(pl.program_id(1) == 0)
    def _():
        if has_state:
            st_ref[...] = s0_ref[...]
            cout_ref[...] = c0_ref[...]
        else:
            st_ref[...] = jnp.zeros_like(st_ref)
            cout_ref[...] = jnp.zeros_like(cout_ref)

    xbc = xbc_ref[...]
    for b in range(nseq):
        base = b * stride
        ext[base + SUBLANES - hist:base + SUBLANES, :] = cout_ref[b]
        ext[base + SUBLANES:base + SUBLANES + lseq, :] = xbc[b * lseq:(b + 1) * lseq, :]
    acc = None
    for j in range(SSM_CONV):
        pieces = [ext[pl.ds(b * stride + SUBLANES - hist + j, lseq), :] for b in range(nseq)]
        sh = pieces[0] if nseq == 1 else jnp.concatenate(pieces, axis=0)
        t = sh * cw_ref[j:j + 1, :]
        acc = t if acc is None else acc + t
    for b in range(nseq):
        base = b * stride
        cout_ref[b] = ext[base + lseq + SUBLANES - hist:base + lseq + SUBLANES, :]
    acc = acc + cb_ref[...]
    conv = acc * _sigmoid(acc)
    x = conv[:, 0:W_GROUP]
    bm = conv[:, W_GROUP:W_GROUP + SSM_GROUPS * gw]
    cm = conv[:, W_GROUP + SSM_GROUPS * gw:]

    pre = dt_ref[...] + par_ref[0:1, :]
    dt = jnp.maximum(pre, 0.0) + jnp.log1p(jnp.exp(-jnp.abs(pre)))
    a = dt * (-jnp.exp(par_ref[1:2, :]))
    cum = _cumsum_rows(a, lseq)
    cum_t = cum.T
    dt_t = dt.T

    rr = _row_iota((rows, rows))
    cc = _lane_iota((rows, rows))
    valid = cc <= rr
    if nseq > 1:
        valid = jnp.logical_and(valid, (rr // lseq) == (cc // lseq))

    erow = _row_iota((LANES, W_GROUP))
    ecol = _lane_iota((LANES, W_GROUP))
    emat = jnp.where(ecol // SSM_P == erow, 1.0, 0.0).astype(BF16)

    lane = _lane_iota((rows, LANES))
    lo_half = lane < SSM_P
    heads_per_group = SSM_HEADS // SSM_GROUPS
    y_chunks = []
    for g in range(SSM_GROUPS):
        gs = slice(g * gw, (g + 1) * gw)
        cb = lax.dot_general(cm[:, gs].astype(BF16), bm[:, gs].astype(BF16), _NT, preferred_element_type=F32)
        for jc in range(heads_per_group // 2):
            chunk = g * (heads_per_group // 2) + jc
            xc = x[:, chunk * LANES:(chunk + 1) * LANES]
            yc = None
            for half in range(2):
                r = 2 * chunk + half
                seg = jnp.broadcast_to(cum[:, r:r + 1], (rows, rows)) - cum_t[r:r + 1, :]
                wts = cb * jnp.exp(jnp.where(valid, seg, NEG_BIG)) * dt_t[r:r + 1, :]
                xm = jnp.where(lo_half, xc, 0.0) if half == 0 else jnp.where(lo_half, 0.0, xc)
                t = _dot(wts, xm)
                yc = t if yc is None else yc + t
            y_chunks.append(yc)
    y = jnp.concatenate(y_chunks, axis=1)

    lastm = _seq_last_rows(cum, nseq, lseq)
    e_cum = _expand_heads(jnp.exp(cum), emat)
    e_wst = _expand_heads(dt * jnp.exp(lastm - cum), emat)
    e_dec = _expand_heads(jnp.exp(lastm), emat)
    xs = x * e_wst
    row5 = _row_iota((rows, W_GROUP))
    inter_groups = []
    pw = heads_per_group * SSM_P
    for g in range(SSM_GROUPS):
        gs = slice(g * gw, (g + 1) * gw)
        ps = slice(g * pw, (g + 1) * pw)
        inter = []
        for b in range(nseq):
            rs = slice(b * lseq, (b + 1) * lseq)
            st = st_ref[b, g]
            inter.append(_dot(cm[rs, gs], st))
            xsb = xs if nseq == 1 else jnp.where(row5 // lseq == b, xs, 0.0)
            upd = _dot_tn(bm[:, gs], xsb[:, ps])
            st_ref[b, g] = st * e_dec[b * lseq:b * lseq + 1, ps] + upd
        inter_groups.append(inter[0] if nseq == 1 else jnp.concatenate(inter, axis=0))
    y = y + jnp.concatenate(inter_groups, axis=1) * e_cum
    y = y + x * par5_ref[0:1, :]
    z = z_ref[...]
    y = y * (z * _sigmoid(z))
    ms = jnp.mean(y * y, axis=-1, keepdims=True)
    y_ref[...] = y * lax.rsqrt(ms + EPS) * par5_ref[1:2, :]


def _ssd_call(grp, proj, cw, cb, par, par5, c0, s0):
    nseq, lseq, no, nt = grp["nseq"], grp["lseq"], grp["no"], grp["nt"]
    has_state = s0 is not None
    ctail = (SSM_CONV - 1, SSM_CONV_DIM)
    stail = (SSM_GROUPS, SSM_N, (SSM_HEADS // SSM_GROUPS) * SSM_P)
    in_specs = [
        _mixer_spec(W_GROUP, COL_CZ, nt), _mixer_spec(SSM_CONV_DIM, COL_XBC, nt), _mixer_spec(LANES, COL_DT, nt),
        _const_spec((SSM_CONV, SSM_CONV_DIM)), _const_spec((1, SSM_CONV_DIM)),
        _const_spec((8, LANES)), _const_spec((8, W_GROUP)),
    ]
    args = [proj, proj, proj, cw, cb.reshape(1, SSM_CONV_DIM), par, par5]
    if has_state:
        in_specs += [_state_spec(ctail, nseq), _state_spec(stail, nseq)]
        args += [c0, s0]
    return pl.pallas_call(
        functools.partial(_ssd_kernel, nseq=nseq, lseq=lseq, has_state=has_state),
        grid=(no, nt),
        in_specs=in_specs,
        out_specs=[_mixer_spec(W_GROUP, 0, nt), _state_spec(ctail, nseq), _state_spec(stail, nseq)],
        out_shape=[jax.ShapeDtypeStruct((grp["m"], W_GROUP), F32),
                   jax.ShapeDtypeStruct((grp["nb"],) + ctail, F32),
                   jax.ShapeDtypeStruct((grp["nb"],) + stail, F32)],
        scratch_shapes=[pltpu.VMEM((nseq * (lseq + SUBLANES), SSM_CONV_DIM), F32)],
        compiler_params=_cparams(("arbitrary", "arbitrary")),
        name="ssd",
    )(*args)


def _head_rmsnorm(x, g, lo_half):
    sq = x * x
    s_lo = jnp.sum(jnp.where(lo_half, sq, 0.0), axis=-1, keepdims=True)
    s_hi = jnp.sum(jnp.where(lo_half, 0.0, sq), axis=-1, keepdims=True)
    ms = jnp.where(lo_half, s_lo, s_hi) * (1.0 / ATTN_DH)
    return x * lax.rsqrt(ms + EPS) * g


def _rope(x, cos, sin_signed, upper_half):
    partner = jnp.where(upper_half, pltpu.roll(x, ATTN_DH // 2, axis=1), pltpu.roll(x, LANES - ATTN_DH // 2, axis=1))
    return x * cos + partner * sin_signed


def _dup_head(x, g, lane):
    own = jnp.where((lane // ATTN_DH) == g, x, 0.0)
    return own + pltpu.roll(own, ATTN_DH, axis=1)


def _swa_kernel(*refs, nseq, lseq, has_state):
    if has_state:
        q_ref, k_ref, v_ref, cos_ref, sin_ref, par_ref, sink_ref, k0_ref, v0_ref, y_ref, kout_ref, vout_ref = refs
    else:
        q_ref, k_ref, v_ref, cos_ref, sin_ref, par_ref, sink_ref, y_ref, kout_ref, vout_ref = refs
        k0_ref = v0_ref = None
    rows = TILE_ROWS
    i = pl.program_id(1)
    rep = ATTN_HEADS // ATTN_KV

    @pl.when(i == 0)
    def _():
        if has_state:
            kout_ref[...] = k0_ref[...]
            vout_ref[...] = v0_ref[...]
        else:
            kout_ref[...] = jnp.zeros_like(kout_ref)
            vout_ref[...] = jnp.zeros_like(vout_ref)

    lane = _lane_iota((rows, LANES))
    lo_half = lane < ATTN_DH
    upper_half = (lane & (ATTN_DH // 2)) != 0
    cos = cos_ref[...]
    sin = sin_ref[...]
    kn = _rope(_head_rmsnorm(k_ref[...], par_ref[1:2, :], lo_half), cos, sin, upper_half)
    vn = v_ref[...]
    scale = ATTN_DH ** -0.5
    qs = []
    for c in range(ATTN_HEADS // 2):
        qc = q_ref[:, c * LANES:(c + 1) * LANES]
        qs.append(_rope(_head_rmsnorm(qc, par_ref[0:1, :], lo_half), cos, sin, upper_half) * scale)

    mq = rep * lseq
    qrow = _row_iota((mq, LANES))
    q_local = qrow & (lseq - 1)
    kcol = _lane_iota((mq, LANES))
    hist_ok = jnp.logical_or(has_state, i > 0)
    hist_valid = jnp.logical_and(kcol > q_local, hist_ok)
    lane_w = _lane_iota((WINDOW, LANES))

    out_rows = [[None] * nseq for _ in range(ATTN_HEADS // 2)]
    for g in range(ATTN_KV):
        k_new = _dup_head(kn, g, lane).astype(BF16)
        v_new = _dup_head(vn, g, lane).astype(BF16)
        for b in range(nseq):
            rs = slice(b * lseq, (b + 1) * lseq)
            q4 = []
            sink_rows = []
            for r in range(rep):
                hidx = g * rep + r
                qc = qs[hidx // 2][rs, :]
                lo_l = lo_half[0:lseq, :]
                q4.append(jnp.where(lo_l, qc, 0.0) if hidx % 2 == 0 else jnp.where(lo_l, 0.0, qc))
                sink_rows.append(jnp.broadcast_to(sink_ref[hidx:hidx + 1, 0:1], (lseq, 1)))
            q4 = jnp.concatenate(q4, axis=0).astype(BF16)
            sink = jnp.concatenate(sink_rows, axis=0)
            k_hist = _dup_head(kout_ref[b], g, lane_w).astype(BF16)
            v_hist = _dup_head(vout_ref[b], g, lane_w).astype(BF16)
            s_h = lax.dot_general(q4, k_hist, _NT, preferred_element_type=F32)
            s_n = lax.dot_general(q4, k_new, _NT, preferred_element_type=F32)
            s_h = jnp.where(hist_valid, s_h, NEG_BIG)
            new_valid = jnp.logical_and(kcol // lseq == b, (kcol & (lseq - 1)) <= q_local)
            s_n = jnp.where(new_valid, s_n, NEG_BIG)
            m = jnp.maximum(jnp.maximum(jnp.max(s_h, axis=-1, keepdims=True),
                                        jnp.max(s_n, axis=-1, keepdims=True)), sink)
            p_h = jnp.exp(s_h - m)
            p_n = jnp.exp(s_n - m)
            den = (jnp.sum(p_h, axis=-1, keepdims=True) + jnp.sum(p_n, axis=-1, keepdims=True)
                   + jnp.exp(sink - m))
            o4 = (jnp.dot(p_h.astype(BF16), v_hist, preferred_element_type=F32)
                  + jnp.dot(p_n.astype(BF16), v_new, preferred_element_type=F32)) / den
            for pair in range(rep // 2):
                c = (g * rep) // 2 + pair
                o_lo = o4[(2 * pair) * lseq:(2 * pair + 1) * lseq, :]
                o_hi = o4[(2 * pair + 1) * lseq:(2 * pair + 2) * lseq, :]
                out_rows[c][b] = jnp.where(lo_half[0:lseq, :], o_lo, o_hi)
    for c in range(ATTN_HEADS // 2):
        y_ref[:, c * LANES:(c + 1) * LANES] = (out_rows[c][0] if nseq == 1 else jnp.concatenate(out_rows[c], axis=0))

    for b in range(nseq):
        rs = slice(b * lseq, (b + 1) * lseq)
        if lseq < WINDOW:
            keep_k = kout_ref[b, lseq:WINDOW, :]
            keep_v = vout_ref[b, lseq:WINDOW, :]
            kout_ref[b, 0:WINDOW - lseq, :] = keep_k
            vout_ref[b, 0:WINDOW - lseq, :] = keep_v
        kout_ref[b, WINDOW - lseq:WINDOW, :] = kn[rs, :]
        vout_ref[b, WINDOW - lseq:WINDOW, :] = vn[rs, :]


def _swa_call(grp, proj, cos, sin, par, sink, k0, v0):
    nseq, lseq, no, nt = grp["nseq"], grp["lseq"], grp["no"], grp["nt"]
    has_state = k0 is not None
    tail = (WINDOW, ATTN_KV * ATTN_DH)
    tab_spec = pl.BlockSpec((TILE_ROWS, LANES), lambda o, i: (i, 0))
    in_specs = [
        _mixer_spec(W_GROUP, COL_DQ, nt), _mixer_spec(LANES, COL_DK, nt), _mixer_spec(LANES, COL_DV, nt),
        tab_spec, tab_spec, _const_spec((8, LANES)), _const_spec((8, LANES)),
    ]
    args = [proj, proj, proj, cos, sin, par, sink]
    if has_state:
        in_specs += [_state_spec(tail, nseq), _state_spec(tail, nseq)]
        args += [k0, v0]
    return pl.pallas_call(
        functools.partial(_swa_kernel, nseq=nseq, lseq=lseq, has_state=has_state),
        grid=(no, nt),
        in_specs=in_specs,
        out_specs=[_mixer_spec(W_GROUP, 0, nt), _state_spec(tail, nseq), _state_spec(tail, nseq)],
        out_shape=[jax.ShapeDtypeStruct((grp["m"], W_GROUP), F32),
                   jax.ShapeDtypeStruct((grp["nb"],) + tail, F32),
                   jax.ShapeDtypeStruct((grp["nb"],) + tail, F32)],
        compiler_params=_cparams(("arbitrary", "arbitrary")),
        name="swa",
    )(*args)


def _rope_tables(pos):
    half = ATTN_DH // 2
    inv = ROPE_THETA ** (-jnp.arange(half, dtype=F32) / half)
    ang = pos.astype(F32)[:, None] * inv[None]
    cos = jnp.tile(jnp.cos(ang), (1, LANES // half))
    sin = jnp.sin(ang)
    sin_signed = jnp.tile(jnp.concatenate([-sin, sin], axis=1), (1, LANES // ATTN_DH))
    return cos, sin_signed


def _pad_lanes(v, width):
    return jnp.pad(v, (0, width - v.shape[0]))


def _rows8(rows_list, width):
    out = jnp.zeros((8, width), F32)
    for r, v in enumerate(rows_list):
        out = out.at[r].set(v)
    return out


def _make_group(nb, seq_len, pos0, mod_rows, tm, tm_out):
    m = nb * seq_len
    if seq_len >= TILE_ROWS:
        nseq, lseq = 1, TILE_ROWS
        no, nt = nb, seq_len // TILE_ROWS
    else:
        nseq, lseq = TILE_ROWS // seq_len, seq_len
        no, nt = m // TILE_ROWS, 1
    per_row = mod_rows.shape[1] != 1

    def mod_spec(kind, tile=tm):
        if per_row:
            def imap(i, *_):
                return (kind, i, 0)
            return pl.BlockSpec((None, tile, D_MODEL), imap)

        def imap(i, *_):
            return ((i * tile) // seq_len * N_MOD + kind, 0, 0)
        return pl.BlockSpec((None, 1, D_MODEL), imap)

    return dict(nb=nb, m=m, tm=tm, tm_out=tm_out, nseq=nseq, lseq=lseq, no=no, nt=nt, pos0=pos0, mod_spec=mod_spec)


def _trunk(grp, x, mods, states, cos, sin, wts):
    (w_in_p, w_out_b, w_up_b, w_down_b, norm1_g, norm2_g, hgrn_par, pool_w_b, pool_scale, conv_w, conv_b,
     ssd_par, ssd_par5, swa_par, sink_par) = wts
    outs = [[] for _ in range(6)]
    for l in range(DEPTH):
        mod = mods[l]
        proj = _inproj_call(grp, x, norm1_g[l], mod, w_in_p[l])
        if states is None:
            s_h = s_p = s_s = s_c = s_k = s_v = None
        else:
            s_h, s_p, s_s, s_c, s_k, s_v = (s[l] for s in states)
        ya, n_h = _hgrn_call(grp, proj, hgrn_par[l], s_h)
        yb, n_p = _pool_call(grp, proj, pool_w_b[l], pool_scale[l], s_p)
        yc, n_c, n_s = _ssd_call(grp, proj, conv_w[l], conv_b[l], ssd_par[l], ssd_par5[l], s_c, s_s)
        yd, n_k, n_v = _swa_call(grp, proj, cos, sin, swa_par[l], sink_par[l], s_k, s_v)
        x = _outproj_call(grp, x, (ya, yb, yc, yd), mod, w_out_b[l])
        x = _mlp_call(grp, x, norm2_g[l], mod, w_up_b[l], w_down_b[l])
        for lst, val in zip(outs, (n_h, n_p, n_s, n_c, n_k, n_v)):
            lst.append(val)
    return x, [jnp.stack(o) for o in outs]


def _ssm_state_to_kernel(s):
    lead = s.shape[:-3]
    r = SSM_HEADS // SSM_GROUPS
    s = s.reshape(lead + (SSM_GROUPS, r, SSM_P, SSM_N))
    s = jnp.moveaxis(s, -1, -3)
    return s.reshape(lead + (SSM_GROUPS, SSM_N, r * SSM_P))


def _ssm_state_from_kernel(s):
    lead = s.shape[:-3]
    r = SSM_HEADS // SSM_GROUPS
    s = s.reshape(lead + (SSM_GROUPS, SSM_N, r, SSM_P))
    s = jnp.moveaxis(s, -3, -1)
    return s.reshape(lead + (SSM_HEADS, SSM_P, SSM_N))


def kernel(x_prompt, x_sample, c_prompt, c_sample, state_hgrn, state_pool, state_ssm, state_conv, cache_k, cache_v, norm1_g, norm2_g, w_ada, b_ada, w_in, hgrn_lb_logits, hgrn_norm_g, pool_w, pool_scale, conv_w, conv_b, dt_bias, a_log, d_skip, ssm_norm_g, q_norm_g, k_norm_g, sinks, w_out, w_up, w_down):
    bp, seq, _ = x_prompt.shape
    bs, dseq, _ = x_sample.shape

    splits = (512, 512, 512, 512, 512, 512, 512, 256, 256, 8, 512, 128, 128)
    offs = [0]
    for s in splits:
        offs.append(offs[-1] + s)
    order = (0, 1, 2, 3, 4, 5, 6, 7, 8, 10, 11, 12)
    cols = [w_in[:, :, offs[k]:offs[k + 1]] for k in order]
    dt_cols = jnp.pad(w_in[:, :, offs[9]:offs[10]], ((0, 0), (0, 0), (0, LANES - SSM_HEADS)))
    w_in_p = jnp.concatenate(cols + [dt_cols], axis=-1).astype(BF16)
    w_out_b = w_out.astype(BF16)
    w_up_b = w_up.astype(BF16)
    w_down_b = w_down.astype(BF16)
    pool_w_b = pool_w.astype(BF16)

    p = jax.nn.softmax(hgrn_lb_logits.astype(F32), axis=0)
    cs = jnp.cumsum(p, axis=0)
    lbs = cs - cs[:1]
    hgrn_par = jnp.stack([_rows8([jnp.log(lbs[l]), jnp.log1p(-lbs[l]), 1.0 - lbs[l], hgrn_norm_g[l]], W_GROUP)
                          for l in range(DEPTH)])
    ssd_par = jnp.stack([_rows8([_pad_lanes(dt_bias[l], LANES), _pad_lanes(a_log[l], LANES)], LANES)
                         for l in range(DEPTH)])
    ssd_par5 = jnp.stack([_rows8([jnp.repeat(d_skip[l], SSM_P), ssm_norm_g[l]], W_GROUP) for l in range(DEPTH)])
    swa_par = jnp.stack([_rows8([jnp.tile(q_norm_g[l], 2), jnp.tile(k_norm_g[l], 2)], LANES) for l in range(DEPTH)])
    sink_par = jnp.broadcast_to(sinks[:, :, None], (DEPTH, ATTN_HEADS, LANES))
    wts = (w_in_p, w_out_b, w_up_b, w_down_b, norm1_g, norm2_g, hgrn_par, pool_w_b, pool_scale, conv_w, conv_b,
           ssd_par, ssd_par5, swa_par, sink_par)

    c_all = jnp.concatenate([c_prompt, c_sample], axis=0)
    mod_all = _ada_call(c_all, w_ada, b_ada)
    mod_p = mod_all[:, :bp].reshape(DEPTH, bp * N_MOD, 1, D_MODEL)
    mod_s = mod_all[:, bp:].reshape(DEPTH, bs, N_MOD, D_MODEL)
    mod_s = jnp.repeat(jnp.moveaxis(mod_s, 2, 1), dseq, axis=2)

    grp_p = _make_group(bp, seq, 0, mod_p[0], 512, 512)
    grp_s = _make_group(bs, dseq, PAST_LEN, mod_s[0], bs * dseq, bs * dseq)

    cos_p, sin_p = _rope_tables(jnp.arange(seq))
    cos_s, sin_s = _rope_tables(PAST_LEN + (jnp.arange(TILE_ROWS) % dseq))

    y_p, st_p = _trunk(grp_p, x_prompt.reshape(bp * seq, D_MODEL), mod_p, None, cos_p, sin_p, wts)

    states = (jnp.swapaxes(state_hgrn, -1, -2), state_pool, _ssm_state_to_kernel(state_ssm), state_conv,
              cache_k.reshape(DEPTH, bs, WINDOW, ATTN_KV * ATTN_DH), cache_v.reshape(DEPTH, bs, WINDOW, ATTN_KV * ATTN_DH))
    y_s, st_s = _trunk(grp_s, x_sample.reshape(bs * dseq, D_MODEL), mod_s, states, cos_s, sin_s, wts)

    def finish(st, nb):
        n_h, n_p, n_s, n_c, n_k, n_v = st
        return (jnp.swapaxes(n_h, -1, -2), n_p, _ssm_state_from_kernel(n_s), n_c,
                n_k.reshape(DEPTH, nb, WINDOW, ATTN_KV, ATTN_DH), n_v.reshape(DEPTH, nb, WINDOW, ATTN_KV, ATTN_DH))

    return ((y_p.reshape(bp, seq, D_MODEL), y_s.reshape(bs, dseq, D_MODEL)) + finish(st_p, bp) + finish(st_s, bs))
```

```python
import functools
import math

import jax
import jax.numpy as jnp
from jax import lax
from jax.experimental import pallas as pl
from jax.experimental.pallas import tpu as pltpu

F32 = jnp.float32
BF16 = jnp.bfloat16

D_MODEL = 2048
DEPTH = 4
PAST_LEN = 16384
W_GROUP = 512
HGRN_HEADS = 4
HGRN_DH = 128
POOL_WINDOWS = (2, 4, 8, 16)
POOL_CH = 128
POOL_BUF = 15
SSM_HEADS = 8
SSM_P = 64
SSM_N = 128
SSM_GROUPS = 2
SSM_CONV = 4
SSM_CONV_DIM = 1024
ATTN_HEADS = 8
ATTN_KV = 2
ATTN_DH = 64
WINDOW = 128
ROPE_THETA = 10000.0
D_FF = 4 * D_MODEL
N_MOD = 6
EPS = 1e-6

LANES = 128
SUBLANES = 8
TILE_ROWS = 128
HIST_PAD = 16
VMEM_LIMIT = 56 * 1024 * 1024

COL_AQ, COL_AF, COL_AI, COL_AG = 0, 512, 1024, 1536
COL_PU, COL_CZ, COL_XBC = 2048, 2560, 3072
MAIN_WIDTH = 4096
COL_DQ, COL_DK, COL_DV, COL_DT = 0, 512, 640, 768
TAIL_WIDTH = 896
NEG_BIG = -1e30

_NT = (((1,), (1,)), ((), ()))
_TN = (((0,), (0,)), ((), ()))


def _dot(a, b):
    return jnp.dot(a.astype(BF16), b.astype(BF16), preferred_element_type=F32)


def _dot_nt(a, b):
    return lax.dot_general(a.astype(BF16), b.astype(BF16), _NT, preferred_element_type=F32)


def _dot_tn(a, b):
    return lax.dot_general(a.astype(BF16), b.astype(BF16), _TN, preferred_element_type=F32)


def _sigmoid(x):
    return 1.0 / (1.0 + jnp.exp(-x))


def _cparams(sem):
    return pltpu.CompilerParams(dimension_semantics=sem, vmem_limit_bytes=VMEM_LIMIT)


def _ada_kernel(c_ref, w_ref, b_ref, o_ref):
    c = c_ref[...]
    s = c * _sigmoid(c)
    o_ref[0] = _dot(s, w_ref[0]) + b_ref[0]


def _ada_call(c_all, w_ada, b_ada):
    rows = c_all.shape[0]
    n = w_ada.shape[-1]
    tn = 1024
    return pl.pallas_call(
        _ada_kernel,
        grid=(DEPTH, n // tn),
        in_specs=[
            pl.BlockSpec((rows, D_MODEL), lambda l, j: (0, 0)),
            pl.BlockSpec((1, D_MODEL, tn), lambda l, j: (l, 0, j)),
            pl.BlockSpec((1, 1, tn), lambda l, j: (l, 0, j)),
        ],
        out_specs=pl.BlockSpec((1, rows, tn), lambda l, j: (l, 0, j)),
        out_shape=jax.ShapeDtypeStruct((DEPTH, rows, n), F32),
        compiler_params=_cparams(("arbitrary", "arbitrary")),
        name="ada_mod",
    )(c_all, w_ada, b_ada.reshape(DEPTH, 1, n))


def _modulated_norm(x, g, scale, shift):
    ms = jnp.mean(x * x, axis=-1, keepdims=True)
    y = x * lax.rsqrt(ms + EPS) * g
    return y * (1.0 + scale) + shift


def _layer_spec(l, shape):
    zeros = (0,) * len(shape)
    return pl.BlockSpec((None,) + tuple(shape), lambda *_: (l,) + zeros)


def _inproj_kernel(x_ref, g_ref, sc_ref, sh_ref, wm_ref, wt_ref, om_ref, ot_ref, h_ref):
    j = pl.program_id(1)
    n_main = pl.num_programs(1) - 1

    @pl.when(j == 0)
    def _():
        h_ref[...] = _modulated_norm(x_ref[...], g_ref[...], sc_ref[...], sh_ref[...]).astype(BF16)

    @pl.when(j < n_main)
    def _():
        om_ref[...] = jnp.dot(h_ref[...], wm_ref[...], preferred_element_type=F32)

    @pl.when(j == n_main)
    def _():
        ot_ref[...] = jnp.dot(h_ref[...], wt_ref[...], preferred_element_type=F32)


def _inproj_call(grp, l, x, g, mod, w_main, w_tail):
    m, tm = grp["m"], grp["tm"]
    tn = 1024
    n_main = MAIN_WIDTH // tn
    return pl.pallas_call(
        _inproj_kernel,
        grid=(m // tm, n_main + 1),
        in_specs=[
            pl.BlockSpec((tm, D_MODEL), lambda i, j: (i, 0)),
            _layer_spec(l, (1, D_MODEL)),
            grp["mod_spec"](l, 1, tm),
            grp["mod_spec"](l, 0, tm),
            pl.BlockSpec((None, D_MODEL, tn), lambda i, j: (l, 0, jnp.minimum(j, n_main - 1))),
            _layer_spec(l, (D_MODEL, TAIL_WIDTH)),
        ],
        out_specs=[pl.BlockSpec((tm, tn), lambda i, j: (i, jnp.minimum(j, n_main - 1))),
                   pl.BlockSpec((tm, TAIL_WIDTH), lambda i, j: (i, 0))],
        out_shape=[jax.ShapeDtypeStruct((m, MAIN_WIDTH), F32), jax.ShapeDtypeStruct((m, TAIL_WIDTH), F32)],
        scratch_shapes=[pltpu.VMEM((tm, D_MODEL), BF16)],
        compiler_params=_cparams(("arbitrary", "arbitrary")),
        name="in_proj",
    )(x, g, mod, mod, w_main, w_tail)


def _outproj_kernel(x_ref, ya_ref, yb_ref, yc_ref, yd_ref, gate_ref, w_ref, o_ref):
    acc = _dot(ya_ref[...], w_ref[0:W_GROUP, :])
    acc += _dot(yb_ref[...], w_ref[W_GROUP:2 * W_GROUP, :])
    acc += _dot(yc_ref[...], w_ref[2 * W_GROUP:3 * W_GROUP, :])
    acc += _dot(yd_ref[...], w_ref[3 * W_GROUP:4 * W_GROUP, :])
    o_ref[...] = x_ref[...] + gate_ref[...] * acc


def _outproj_call(grp, l, x, ys, mod, w):
    m, tm = grp["m"], grp["tm_out"]
    yspec = pl.BlockSpec((tm, W_GROUP), lambda i: (i, 0))
    return pl.pallas_call(
        _outproj_kernel,
        grid=(m // tm,),
        in_specs=[
            pl.BlockSpec((tm, D_MODEL), lambda i: (i, 0)),
            yspec, yspec, yspec, yspec,
            grp["mod_spec"](l, 2, tm),
            _layer_spec(l, (D_MODEL, D_MODEL)),
        ],
        out_specs=pl.BlockSpec((tm, D_MODEL), lambda i: (i, 0)),
        out_shape=jax.ShapeDtypeStruct((m, D_MODEL), F32),
        compiler_params=_cparams(("arbitrary",)),
        name="out_proj",
    )(x, *ys, mod, w)


def _mlp_kernel(x_ref, g_ref, sc_ref, sh_ref, gate_ref, wu_ref, wd_ref, o_ref, h_ref):
    j = pl.program_id(1)

    @pl.when(j == 0)
    def _():
        h_ref[...] = _modulated_norm(x_ref[...], g_ref[...], sc_ref[...], sh_ref[...]).astype(BF16)
        o_ref[...] = jnp.zeros_like(o_ref)

    u = jnp.dot(h_ref[...], wu_ref[...].astype(BF16), preferred_element_type=F32)
    a = jnp.square(jnp.maximum(u, 0.0))
    o_ref[...] += jnp.dot(a.astype(BF16), wd_ref[...].astype(BF16), preferred_element_type=F32)

    @pl.when(j == pl.num_programs(1) - 1)
    def _():
        o_ref[...] = x_ref[...] + gate_ref[...] * o_ref[...]


def _mlp_call(grp, l, x, g, mod, wu, wd):
    m, tm = grp["m"], grp["tm"]
    tf = 512
    return pl.pallas_call(
        _mlp_kernel,
        grid=(m // tm, D_FF // tf),
        in_specs=[
            pl.BlockSpec((tm, D_MODEL), lambda i, j: (i, 0), pipeline_mode=pl.Buffered(1)),
            _layer_spec(l, (1, D_MODEL)),
            grp["mod_spec"](l, 4, tm),
            grp["mod_spec"](l, 3, tm),
            grp["mod_spec"](l, 5, tm),
            pl.BlockSpec((None, D_MODEL, tf), lambda i, j: (l, 0, j)),
            pl.BlockSpec((None, tf, D_MODEL), lambda i, j: (l, j, 0)),
        ],
        out_specs=pl.BlockSpec((tm, D_MODEL), lambda i, j: (i, 0)),
        out_shape=jax.ShapeDtypeStruct((m, D_MODEL), F32),
        scratch_shapes=[pltpu.VMEM((tm, D_MODEL), BF16)],
        compiler_params=_cparams(("arbitrary", "arbitrary")),
        name="mlp",
    )(x, g, mod, mod, mod, wu, wd)


def _row_iota(shape):
    return lax.broadcasted_iota(jnp.int32, shape, 0)


def _lane_iota(shape):
    return lax.broadcasted_iota(jnp.int32, shape, 1)


def _cumsum_rows(x, lseq):
    pos = _row_iota(x.shape) & (lseq - 1)
    s = 1
    while s < lseq:
        x = x + jnp.where(pos >= s, pltpu.roll(x, s, axis=0), 0.0)
        s *= 2
    return x


def _seq_last_rows(x, nseq, lseq):
    pieces = [jnp.broadcast_to(x[(b + 1) * lseq - 1:(b + 1) * lseq, :], (lseq, x.shape[1])) for b in range(nseq)]
    return pieces[0] if nseq == 1 else jnp.concatenate(pieces, axis=0)


def _mixer_spec(width, col, nt):
    return pl.BlockSpec((TILE_ROWS, width), lambda o, i: (o * nt + i, col // width))


def _state_spec(shape_tail, nseq, l=None):
    zeros = (0,) * len(shape_tail)
    if l is None:
        return pl.BlockSpec((nseq,) + tuple(shape_tail), lambda o, i: (o,) + zeros)
    return pl.BlockSpec((None, nseq) + tuple(shape_tail), lambda o, i: (l, o) + zeros)


def _hgrn_kernel(*refs, nseq, lseq, has_state):
    if has_state:
        aq_ref, af_ref, ai_ref, ag_ref, par_ref, s0_ref, y_ref, st_ref, kpad, cpad, vpad = refs
    else:
        aq_ref, af_ref, ai_ref, ag_ref, par_ref, y_ref, st_ref, kpad, cpad, vpad = refs
        s0_ref = None
    rows = TILE_ROWS
    dh = HGRN_DH

    @pl.when(pl.program_id(1) == 0)
    def _():
        if has_state:
            st_ref[...] = s0_ref[...]
        else:
            st_ref[...] = jnp.zeros_like(st_ref)

    zpad = jnp.zeros((HIST_PAD, W_GROUP), F32)
    kpad[0:HIST_PAD, :] = zpad
    cpad[0:HIST_PAD, :] = zpad
    vpad[0:HIST_PAD, :] = zpad

    aq = aq_ref[...]
    xf = af_ref[...]
    v = ai_ref[...]
    q = aq * _sigmoid(aq)
    e = jnp.exp(-jnp.abs(xf))
    inv = 1.0 / (1.0 + e)
    log_sig = jnp.minimum(xf, 0.0) - jnp.log1p(e)
    log_lb = par_ref[0:1, :]
    bterm = par_ref[1:2, :] + log_sig
    lf = jnp.maximum(log_lb, bterm) + jnp.log1p(jnp.exp(-jnp.abs(log_lb - bterm)))
    k = par_ref[2:3, :] * jnp.where(xf >= 0.0, e * inv, inv)

    cum = _cumsum_rows(lf, lseq)
    kpad[HIST_PAD:, :] = k
    cpad[HIST_PAD:, :] = cum
    vpad[HIST_PAD:, :] = v

    row = _row_iota((rows, W_GROUP))
    o = [jnp.zeros((rows, dh), F32) for _ in range(HGRN_HEADS)]

    sub = min(16, lseq)
    pos_sub = row & (sub - 1)
    for d in range(sub):
        if d == 0:
            term = q * k
            vs = v
        else:
            ks = kpad[pl.ds(HIST_PAD - d, rows), :]
            cs = cpad[pl.ds(HIST_PAD - d, rows), :]
            vs = vpad[pl.ds(HIST_PAD - d, rows), :]
            term = q * ks * jnp.exp(jnp.where(pos_sub >= d, cum - cs, NEG_BIG))
        for hd in range(HGRN_HEADS):
            sl = slice(hd * dh, (hd + 1) * dh)
            sc = jnp.sum(term[:, sl], axis=-1, keepdims=True)
            o[hd] = o[hd] + sc * vs[:, sl]

    levels = []
    h = lseq // 2
    while h >= sub:
        levels.append(h)
        h //= 2
    if levels:
        rr = _row_iota((rows, rows))
        cc = _lane_iota((rows, rows))
        p = [jnp.zeros((rows, rows), F32) for _ in range(HGRN_HEADS)]
        for h in levels:
            upper = (row & h) != 0
            pieces = []
            for jb in range(rows // (2 * h)):
                mid = jb * 2 * h + h
                pieces.append(jnp.broadcast_to(cpad[pl.ds(HIST_PAD + mid - 1, 1), :], (2 * h, W_GROUP)))
            refm = pieces[0] if len(pieces) == 1 else jnp.concatenate(pieces, axis=0)
            x = jnp.exp(jnp.where(upper, cum - refm, refm - cum))
            a_side = jnp.where(upper, q * x, 0.0).astype(BF16)
            b_side = jnp.where(upper, 0.0, k * x).astype(BF16)
            if 2 * h < rows:
                same = (rr // (2 * h)) == (cc // (2 * h))
            for hd in range(HGRN_HEADS):
                sl = slice(hd * dh, (hd + 1) * dh)
                s = lax.dot_general(a_side[:, sl], b_side[:, sl], _NT, preferred_element_type=F32)
                if 2 * h < rows:
                    s = jnp.where(same, s, 0.0)
                p[hd] = p[hd] + s
        for hd in range(HGRN_HEADS):
            sl = slice(hd * dh, (hd + 1) * dh)
            o[hd] = o[hd] + _dot(p[hd], v[:, sl])

    qe = (q * jnp.exp(cum)).astype(BF16)
    lastm = _seq_last_rows(cum, nseq, lseq)
    kd = k * jnp.exp(lastm - cum)
    dec = jnp.exp(lastm)
    vb = v.astype(BF16)
    seq_of_row = row // lseq
    for hd in range(HGRN_HEADS):
        sl = slice(hd * dh, (hd + 1) * dh)
        inter = []
        for b in range(nseq):
            rs = slice(b * lseq, (b + 1) * lseq)
            st = st_ref[b, hd]
            inter.append(lax.dot_general(qe[rs, sl], st.astype(BF16), _NT, preferred_element_type=F32))
            kdb = kd if nseq == 1 else jnp.where(seq_of_row == b, kd, 0.0)
            upd = lax.dot_general(vb[:, sl], kdb[:, sl].astype(BF16), _TN, preferred_element_type=F32)
            st_ref[b, hd] = st * dec[b * lseq:b * lseq + 1, sl] + upd
        o[hd] = o[hd] + (inter[0] if nseq == 1 else jnp.concatenate(inter, axis=0))

    ag = ag_ref[...]
    gate = _sigmoid(ag)
    for hd in range(HGRN_HEADS):
        sl = slice(hd * dh, (hd + 1) * dh)
        ms = jnp.mean(o[hd] * o[hd], axis=-1, keepdims=True)
        y_ref[:, sl] = o[hd] * lax.rsqrt(ms + EPS) * par_ref[3:4, sl] * gate[:, sl]


def _hgrn_call(grp, l, proj, par, s0):
    nseq, lseq, no, nt = grp["nseq"], grp["lseq"], grp["no"], grp["nt"]
    has_state = s0 is not None
    st_tail = (HGRN_HEADS, HGRN_DH, HGRN_DH)
    in_specs = ([_mixer_spec(W_GROUP, c, nt) for c in (COL_AQ, COL_AF, COL_AI, COL_AG)]
                + [_layer_spec(l, (8, W_GROUP))])
    args = [proj, proj, proj, proj, par]
    if has_state:
        in_specs.append(_state_spec(st_tail, nseq, l))
        args.append(s0)
    pad = pltpu.VMEM((HIST_PAD + TILE_ROWS, W_GROUP), F32)
    return pl.pallas_call(
        functools.partial(_hgrn_kernel, nseq=nseq, lseq=lseq, has_state=has_state),
        grid=(no, nt),
        in_specs=in_specs,
        out_specs=[_mixer_spec(W_GROUP, 0, nt), _state_spec(st_tail, nseq)],
        out_shape=[jax.ShapeDtypeStruct((grp["m"], W_GROUP), F32),
                   jax.ShapeDtypeStruct((grp["nb"],) + st_tail, F32)],
        scratch_shapes=[pad, pad, pad],
        compiler_params=_cparams(("arbitrary", "arbitrary")),
        name="hgrn",
    )(*args)


def _pool_kernel(*refs, nseq, lseq, has_state, pos0):
    if has_state:
        u_ref, pw_ref, sc_ref, h0_ref, y_ref, hout_ref, ext = refs
    else:
        u_ref, pw_ref, sc_ref, y_ref, hout_ref, ext = refs
        h0_ref = None
    i = pl.program_id(1)
    stride = lseq + HIST_PAD

    @pl.when(i == 0)
    def _():
        if has_state:
            hout_ref[...] = h0_ref[...]
        else:
            hout_ref[...] = jnp.zeros_like(hout_ref)

    u = u_ref[...]
    for b in range(nseq):
        base = b * stride
        ext[base + HIST_PAD - POOL_BUF:base + HIST_PAD, :] = hout_ref[b]
        ext[base + HIST_PAD:base + HIST_PAD + lseq, :] = u[b * lseq:(b + 1) * lseq, :]

    local = _row_iota((TILE_ROWS, POOL_CH)) & (lseq - 1)
    posn = pos0 + i * lseq + local
    for gi, win in enumerate(POOL_WINDOWS):
        cs = slice(gi * POOL_CH, (gi + 1) * POOL_CH)
        pieces = []
        for b in range(nseq):
            base = b * stride + HIST_PAD
            s = ext[base:base + lseq, cs]
            for j in range(1, win):
                s = s + ext[pl.ds(base - j, lseq), cs]
            pieces.append(s)
        s = pieces[0] if nseq == 1 else jnp.concatenate(pieces, axis=0)
        cnt = jnp.minimum(posn + 1, win).astype(F32)
        pooled = s / cnt - u[:, cs]
        y_ref[:, cs] = _dot(pooled, pw_ref[gi]) * sc_ref[:, cs]

    for b in range(nseq):
        base = b * stride
        hout_ref[b] = ext[base + lseq + HIST_PAD - POOL_BUF:base + lseq + HIST_PAD, :]


def _pool_call(grp, l, proj, pw, scale, h0):
    nseq, lseq, no, nt = grp["nseq"], grp["lseq"], grp["no"], grp["nt"]
    has_state = h0 is not None
    tail = (POOL_BUF, W_GROUP)
    in_specs = [_mixer_spec(W_GROUP, COL_PU, nt), _layer_spec(l, (4, POOL_CH, POOL_CH)), _layer_spec(l, (1, W_GROUP))]
    args = [proj, pw, scale]
    if has_state:
        in_specs.append(_state_spec(tail, nseq, l))
        args.append(h0)
    return pl.pallas_call(
        functools.partial(_pool_kernel, nseq=nseq, lseq=lseq, has_state=has_state, pos0=grp["pos0"]),
        grid=(no, nt),
        in_specs=in_specs,
        out_specs=[_mixer_spec(W_GROUP, 0, nt), _state_spec(tail, nseq)],
        out_shape=[jax.ShapeDtypeStruct((grp["m"], W_GROUP), F32),
                   jax.ShapeDtypeStruct((grp["nb"],) + tail, F32)],
        scratch_shapes=[pltpu.VMEM((nseq * (lseq + HIST_PAD), W_GROUP), F32)],
        compiler_params=_cparams(("arbitrary", "arbitrary")),
        name="pool",
    )(*args)


def _expand_heads(z, emat):
    hi = z.astype(BF16)
    lo = (z - hi.astype(F32)).astype(BF16)
    return (jnp.dot(hi, emat, preferred_element_type=F32) + jnp.dot(lo, emat, preferred_element_type=F32))


def _ssd_kernel(*refs, nseq, lseq, has_state):
    if has_state:
        (z_ref, xbc_ref, dt_ref, cw_ref, cb_ref, par_ref, par5_ref, c0_ref, s0_ref,
         y_ref, cout_ref, st_ref, ext) = refs
    else:
        (z_ref, xbc_ref, dt_ref, cw_ref, cb_ref, par_ref, par5_ref,
         y_ref, cout_ref, st_ref, ext) = refs
        c0_ref = s0_ref = None
    rows = TILE_ROWS
    hist = SSM_CONV - 1
    stride = lseq + SUBLANES
    gw = SSM_N

    @pl.when(pl.program_id(1) == 0)
    def _():
        if has_state:
            st_ref[...] = s0_ref[...]
            cout_ref[...] = c0_ref[...]
        else:
            st_ref[...] = jnp.zeros_like(st_ref)
            cout_ref[...] = jnp.zeros_like(cout_ref)

    xbc = xbc_ref[...]
    for b in range(nseq):
        base = b * stride
        ext[base + SUBLANES - hist:base + SUBLANES, :] = cout_ref[b]
        ext[base + SUBLANES:base + SUBLANES + lseq, :] = xbc[b * lseq:(b + 1) * lseq, :]
    acc = None
    for j in range(SSM_CONV):
        pieces = [ext[pl.ds(b * stride + SUBLANES - hist + j, lseq), :] for b in range(nseq)]
        sh = pieces[0] if nseq == 1 else jnp.concatenate(pieces, axis=0)
        t = sh * cw_ref[j:j + 1, :]
        acc = t if acc is None else acc + t
    for b in range(nseq):
        base = b * stride
        cout_ref[b] = ext[base + lseq + SUBLANES - hist:base + lseq + SUBLANES, :]
    acc = acc + cb_ref[...]
    conv = acc * _sigmoid(acc)
    x = conv[:, 0:W_GROUP]
    bm = conv[:, W_GROUP:W_GROUP + SSM_GROUPS * gw]
    cm = conv[:, W_GROUP + SSM_GROUPS * gw:]

    pre = dt_ref[...] + par_ref[0:1, :]
    dt = jnp.maximum(pre, 0.0) + jnp.log1p(jnp.exp(-jnp.abs(pre)))
    a = dt * (-jnp.exp(par_ref[1:2, :]))
    cum = _cumsum_rows(a, lseq)
    cum_t = cum.T
    dt_t = dt.T

    rr = _row_iota((rows, rows))
    cc = _lane_iota((rows, rows))
    valid = cc <= rr
    if nseq > 1:
        valid = jnp.logical_and(valid, (rr // lseq) == (cc // lseq))

    erow = _row_iota((LANES, W_GROUP))
    ecol = _lane_iota((LANES, W_GROUP))
    emat = jnp.where(ecol // SSM_P == erow, 1.0, 0.0).astype(BF16)

    lane = _lane_iota((rows, LANES))
    lo_half = lane < SSM_P
    heads_per_group = SSM_HEADS // SSM_GROUPS
    y_chunks = []
    for g in range(SSM_GROUPS):
        gs = slice(g * gw, (g + 1) * gw)
        cb = lax.dot_general(cm[:, gs].astype(BF16), bm[:, gs].astype(BF16), _NT, preferred_element_type=F32)
        for jc in range(heads_per_group // 2):
            chunk = g * (heads_per_group // 2) + jc
            xc = x[:, chunk * LANES:(chunk + 1) * LANES]
            yc = None
            for half in range(2):
                r = 2 * chunk + half
                seg = jnp.broadcast_to(cum[:, r:r + 1], (rows, rows)) - cum_t[r:r + 1, :]
                wts = cb * jnp.exp(jnp.where(valid, seg, NEG_BIG)) * dt_t[r:r + 1, :]
                xm = jnp.where(lo_half, xc, 0.0) if half == 0 else jnp.where(lo_half, 0.0, xc)
                t = _dot(wts, xm)
                yc = t if yc is None else yc + t
            y_chunks.append(yc)
    y = jnp.concatenate(y_chunks, axis=1)

    lastm = _seq_last_rows(cum, nseq, lseq)
    e_cum = _expand_heads(jnp.exp(cum), emat)
    e_wst = _expand_heads(dt * jnp.exp(lastm - cum), emat)
    e_dec = _expand_heads(jnp.exp(lastm), emat)
    xs = x * e_wst
    row5 = _row_iota((rows, W_GROUP))
    inter_groups = []
    pw = heads_per_group * SSM_P
    for g in range(SSM_GROUPS):
        gs = slice(g * gw, (g + 1) * gw)
        ps = slice(g * pw, (g + 1) * pw)
        inter = []
        for b in range(nseq):
            rs = slice(b * lseq, (b + 1) * lseq)
            st = st_ref[b, g]
            inter.append(_dot(cm[rs, gs], st))
            xsb = xs if nseq == 1 else jnp.where(row5 // lseq == b, xs, 0.0)
            upd = _dot_tn(bm[:, gs], xsb[:, ps])
            st_ref[b, g] = st * e_dec[b * lseq:b * lseq + 1, ps] + upd
        inter_groups.append(inter[0] if nseq == 1 else jnp.concatenate(inter, axis=0))
    y = y + jnp.concatenate(inter_groups, axis=1) * e_cum
    y = y + x * par5_ref[0:1, :]
    z = z_ref[...]
    y = y * (z * _sigmoid(z))
    ms = jnp.mean(y * y, axis=-1, keepdims=True)
    y_ref[...] = y * lax.rsqrt(ms + EPS) * par5_ref[1:2, :]


def _ssd_call(grp, l, proj, tail, cw, cb, par, par5, c0, s0):
    nseq, lseq, no, nt = grp["nseq"], grp["lseq"], grp["no"], grp["nt"]
    has_state = s0 is not None
    ctail = (SSM_CONV - 1, SSM_CONV_DIM)
    stail = (SSM_GROUPS, SSM_N, (SSM_HEADS // SSM_GROUPS) * SSM_P)
    in_specs = [
        _mixer_spec(W_GROUP, COL_CZ, nt), _mixer_spec(SSM_CONV_DIM, COL_XBC, nt), _mixer_spec(LANES, COL_DT, nt),
        _layer_spec(l, (SSM_CONV, SSM_CONV_DIM)), _layer_spec(l, (1, SSM_CONV_DIM)),
        _layer_spec(l, (8, LANES)), _layer_spec(l, (8, W_GROUP)),
    ]
    args = [proj, proj, tail, cw, cb, par, par5]
    if has_state:
        in_specs += [_state_spec(ctail, nseq, l), _state_spec(stail, nseq, l)]
        args += [c0, s0]
    return pl.pallas_call(
        functools.partial(_ssd_kernel, nseq=nseq, lseq=lseq, has_state=has_state),
        grid=(no, nt),
        in_specs=in_specs,
        out_specs=[_mixer_spec(W_GROUP, 0, nt), _state_spec(ctail, nseq), _state_spec(stail, nseq)],
        out_shape=[jax.ShapeDtypeStruct((grp["m"], W_GROUP), F32),
                   jax.ShapeDtypeStruct((grp["nb"],) + ctail, F32),
                   jax.ShapeDtypeStruct((grp["nb"],) + stail, F32)],
        scratch_shapes=[pltpu.VMEM((nseq * (lseq + SUBLANES), SSM_CONV_DIM), F32)],
        compiler_params=_cparams(("arbitrary", "arbitrary")),
        name="ssd",
    )(*args)


def _head_rmsnorm(x, g, lo_half):
    sq = x * x
    s_lo = jnp.sum(jnp.where(lo_half, sq, 0.0), axis=-1, keepdims=True)
    s_hi = jnp.sum(jnp.where(lo_half, 0.0, sq), axis=-1, keepdims=True)
    ms = jnp.where(lo_half, s_lo, s_hi) * (1.0 / ATTN_DH)
    return x * lax.rsqrt(ms + EPS) * g


def _rope(x, cos, sin_signed, upper_half):
    partner = jnp.where(upper_half, pltpu.roll(x, ATTN_DH // 2, axis=1), pltpu.roll(x, LANES - ATTN_DH // 2, axis=1))
    return x * cos + partner * sin_signed


def _dup_head(x, g, lane):
    own = jnp.where((lane // ATTN_DH) == g, x, 0.0)
    return own + pltpu.roll(own, ATTN_DH, axis=1)


def _swa_kernel(*refs, nseq, lseq, has_state):
    if has_state:
        q_ref, k_ref, v_ref, cos_ref, sin_ref, par_ref, sink_ref, k0_ref, v0_ref, y_ref, kout_ref, vout_ref = refs
    else:
        q_ref, k_ref, v_ref, cos_ref, sin_ref, par_ref, sink_ref, y_ref, kout_ref, vout_ref = refs
        k0_ref = v0_ref = None
    rows = TILE_ROWS
    i = pl.program_id(1)
    rep = ATTN_HEADS // ATTN_KV

    @pl.when(i == 0)
    def _():
        if has_state:
            kout_ref[...] = k0_ref[...]
            vout_ref[...] = v0_ref[...]
        else:
            kout_ref[...] = jnp.zeros_like(kout_ref)
            vout_ref[...] = jnp.zeros_like(vout_ref)

    lane = _lane_iota((rows, LANES))
    lo_half = lane < ATTN_DH
    upper_half = (lane & (ATTN_DH // 2)) != 0
    cos = cos_ref[...]
    sin = sin_ref[...]
    kn = _rope(_head_rmsnorm(k_ref[...], par_ref[1:2, :], lo_half), cos, sin, upper_half)
    vn = v_ref[...]
    scale = ATTN_DH ** -0.5
    qs = []
    for c in range(ATTN_HEADS // 2):
        qc = q_ref[:, c * LANES:(c + 1) * LANES]
        qs.append(_rope(_head_rmsnorm(qc, par_ref[0:1, :], lo_half), cos, sin, upper_half) * scale)

    mq = rep * lseq
    qrow = _row_iota((mq, LANES))
    q_local = qrow & (lseq - 1)
    kcol = _lane_iota((mq, LANES))
    hist_ok = jnp.logical_or(has_state, i > 0)
    hist_valid = jnp.logical_and(kcol > q_local, hist_ok)
    lane_w = _lane_iota((WINDOW, LANES))

    out_rows = [[None] * nseq for _ in range(ATTN_HEADS // 2)]
    for g in range(ATTN_KV):
        k_new = _dup_head(kn, g, lane).astype(BF16)
        v_new = _dup_head(vn, g, lane).astype(BF16)
        for b in range(nseq):
            rs = slice(b * lseq, (b + 1) * lseq)
            q4 = []
            sink_rows = []
            for r in range(rep):
                hidx = g * rep + r
                qc = qs[hidx // 2][rs, :]
                lo_l = lo_half[0:lseq, :]
                q4.append(jnp.where(lo_l, qc, 0.0) if hidx % 2 == 0 else jnp.where(lo_l, 0.0, qc))
                sink_rows.append(jnp.broadcast_to(sink_ref[hidx:hidx + 1, 0:1], (lseq, 1)))
            q4 = jnp.concatenate(q4, axis=0).astype(BF16)
            sink = jnp.concatenate(sink_rows, axis=0)
            k_hist = _dup_head(kout_ref[b], g, lane_w).astype(BF16)
            v_hist = _dup_head(vout_ref[b], g, lane_w).astype(BF16)
            s_h = lax.dot_general(q4, k_hist, _NT, preferred_element_type=F32)
            s_n = lax.dot_general(q4, k_new, _NT, preferred_element_type=F32)
            s_h = jnp.where(hist_valid, s_h, NEG_BIG)
            new_valid = jnp.logical_and(kcol // lseq == b, (kcol & (lseq - 1)) <= q_local)
            s_n = jnp.where(new_valid, s_n, NEG_BIG)
            m = jnp.maximum(jnp.maximum(jnp.max(s_h, axis=-1, keepdims=True),
                                        jnp.max(s_n, axis=-1, keepdims=True)), sink)
            p_h = jnp.exp(s_h - m)
            p_n = jnp.exp(s_n - m)
            den = (jnp.sum(p_h, axis=-1, keepdims=True) + jnp.sum(p_n, axis=-1, keepdims=True)
                   + jnp.exp(sink - m))
            o4 = (jnp.dot(p_h.astype(BF16), v_hist, preferred_element_type=F32)
                  + jnp.dot(p_n.astype(BF16), v_new, preferred_element_type=F32)) / den
            for pair in range(rep // 2):
                c = (g * rep) // 2 + pair
                o_lo = o4[(2 * pair) * lseq:(2 * pair + 1) * lseq, :]
                o_hi = o4[(2 * pair + 1) * lseq:(2 * pair + 2) * lseq, :]
                out_rows[c][b] = jnp.where(lo_half[0:lseq, :], o_lo, o_hi)
    for c in range(ATTN_HEADS // 2):
        y_ref[:, c * LANES:(c + 1) * LANES] = (out_rows[c][0] if nseq == 1 else jnp.concatenate(out_rows[c], axis=0))

    for b in range(nseq):
        rs = slice(b * lseq, (b + 1) * lseq)
        if lseq < WINDOW:
            keep_k = kout_ref[b, lseq:WINDOW, :]
            keep_v = vout_ref[b, lseq:WINDOW, :]
            kout_ref[b, 0:WINDOW - lseq, :] = keep_k
            vout_ref[b, 0:WINDOW - lseq, :] = keep_v
        kout_ref[b, WINDOW - lseq:WINDOW, :] = kn[rs, :]
        vout_ref[b, WINDOW - lseq:WINDOW, :] = vn[rs, :]


def _swa_call(grp, l, proj, cos, sin, par, sink, k0, v0):
    nseq, lseq, no, nt = grp["nseq"], grp["lseq"], grp["no"], grp["nt"]
    has_state = k0 is not None
    tail = (WINDOW, ATTN_KV * ATTN_DH)
    tab_spec = pl.BlockSpec((TILE_ROWS, LANES), lambda o, i: (i, 0))
    in_specs = [
        _mixer_spec(W_GROUP, COL_DQ, nt), _mixer_spec(LANES, COL_DK, nt), _mixer_spec(LANES, COL_DV, nt),
        tab_spec, tab_spec, _layer_spec(l, (8, LANES)), _layer_spec(l, (8, LANES)),
    ]
    args = [proj, proj, proj, cos, sin, par, sink]
    if has_state:
        in_specs += [_state_spec(tail, nseq, l), _state_spec(tail, nseq, l)]
        args += [k0, v0]
    return pl.pallas_call(
        functools.partial(_swa_kernel, nseq=nseq, lseq=lseq, has_state=has_state),
        grid=(no, nt),
        in_specs=in_specs,
        out_specs=[_mixer_spec(W_GROUP, 0, nt), _state_spec(tail, nseq), _state_spec(tail, nseq)],
        out_shape=[jax.ShapeDtypeStruct((grp["m"], W_GROUP), F32),
                   jax.ShapeDtypeStruct((grp["nb"],) + tail, F32),
                   jax.ShapeDtypeStruct((grp["nb"],) + tail, F32)],
        compiler_params=_cparams(("arbitrary", "arbitrary")),
        name="swa",
    )(*args)


def _rope_tables(pos):
    half = ATTN_DH // 2
    inv = ROPE_THETA ** (-jnp.arange(half, dtype=F32) / half)
    ang = pos.astype(F32)[:, None] * inv[None]
    cos = jnp.tile(jnp.cos(ang), (1, LANES // half))
    sin = jnp.sin(ang)
    sin_signed = jnp.tile(jnp.concatenate([-sin, sin], axis=1), (1, LANES // ATTN_DH))
    return cos, sin_signed


def _pad_lanes(v, width):
    return jnp.pad(v, (0, width - v.shape[0]))


def _rows8(rows_list, width):
    out = jnp.zeros((8, width), F32)
    for r, v in enumerate(rows_list):
        out = out.at[r].set(v)
    return out


def _make_group(nb, seq_len, pos0, mod_rows, tm, tm_out):
    m = nb * seq_len
    if seq_len >= TILE_ROWS:
        nseq, lseq = 1, TILE_ROWS
        no, nt = nb, seq_len // TILE_ROWS
    else:
        nseq, lseq = TILE_ROWS // seq_len, seq_len
        no, nt = m // TILE_ROWS, 1
    per_row = mod_rows.shape[-2] != 1

    def mod_spec(l, kind, tile):
        if per_row:
            def imap(i, *_):
                return (l, kind, i, 0)
            return pl.BlockSpec((None, None, tile, D_MODEL), imap)

        def imap(i, *_):
            return (l, (i * tile) // seq_len * N_MOD + kind, 0, 0)
        return pl.BlockSpec((None, None, 1, D_MODEL), imap)

    return dict(nb=nb, m=m, tm=tm, tm_out=tm_out, nseq=nseq, lseq=lseq, no=no, nt=nt, pos0=pos0, mod_spec=mod_spec)


def _trunk(grp, x, mod, states, cos, sin, wts):
    (w_main, w_tail, w_out_b, w_up, w_down, norm1_g, norm2_g, hgrn_par, pool_w_b, pool_scale, conv_w, conv_b,
     ssd_par, ssd_par5, swa_par, sink_par) = wts
    outs = [[] for _ in range(6)]
    if states is None:
        s_h = s_p = s_s = s_c = s_k = s_v = None
    else:
        s_h, s_p, s_s, s_c, s_k, s_v = states
    for l in range(DEPTH):
        proj, tail = _inproj_call(grp, l, x, norm1_g, mod, w_main, w_tail)
        ya, n_h = _hgrn_call(grp, l, proj, hgrn_par, s_h)
        yb, n_p = _pool_call(grp, l, proj, pool_w_b, pool_scale, s_p)
        yc, n_c, n_s = _ssd_call(grp, l, proj, tail, conv_w, conv_b, ssd_par, ssd_par5, s_c, s_s)
        yd, n_k, n_v = _swa_call(grp, l, tail, cos, sin, swa_par, sink_par, s_k, s_v)
        x = _outproj_call(grp, l, x, (ya, yb, yc, yd), mod, w_out_b)
        x = _mlp_call(grp, l, x, norm2_g, mod, w_up, w_down)
        for lst, val in zip(outs, (n_h, n_p, n_s, n_c, n_k, n_v)):
            lst.append(val)
    return x, [jnp.stack(o) for o in outs]


def _ssm_state_to_kernel(s):
    lead = s.shape[:-3]
    r = SSM_HEADS // SSM_GROUPS
    s = s.reshape(lead + (SSM_GROUPS, r, SSM_P, SSM_N))
    s = jnp.moveaxis(s, -1, -3)
    return s.reshape(lead + (SSM_GROUPS, SSM_N, r * SSM_P))


def _ssm_state_from_kernel(s):
    lead = s.shape[:-3]
    r = SSM_HEADS // SSM_GROUPS
    s = s.reshape(lead + (SSM_GROUPS, SSM_N, r, SSM_P))
    s = jnp.moveaxis(s, -3, -1)
    return s.reshape(lead + (SSM_HEADS, SSM_P, SSM_N))


def kernel(x_prompt, x_sample, c_prompt, c_sample, state_hgrn, state_pool, state_ssm, state_conv, cache_k, cache_v, norm1_g, norm2_g, w_ada, b_ada, w_in, hgrn_lb_logits, hgrn_norm_g, pool_w, pool_scale, conv_w, conv_b, dt_bias, a_log, d_skip, ssm_norm_g, q_norm_g, k_norm_g, sinks, w_out, w_up, w_down):
    bp, seq, _ = x_prompt.shape
    bs, dseq, _ = x_sample.shape

    w_main = w_in[:, :, :MAIN_WIDTH].astype(BF16)
    dt_end = MAIN_WIDTH + SSM_HEADS
    w_tail = jnp.concatenate(
        [w_in[:, :, dt_end:], w_in[:, :, MAIN_WIDTH:dt_end],
         jnp.zeros((DEPTH, D_MODEL, LANES - SSM_HEADS), F32)], axis=-1).astype(BF16)
    w_out_b = w_out.astype(BF16)
    pool_w_b = pool_w.astype(BF16)

    p = jax.nn.softmax(hgrn_lb_logits.astype(F32), axis=0)
    cs = jnp.cumsum(p, axis=0)
    lbs = cs - cs[:1]
    hgrn_par = jnp.stack([_rows8([jnp.log(lbs[l]), jnp.log1p(-lbs[l]), 1.0 - lbs[l], hgrn_norm_g[l]], W_GROUP)
                          for l in range(DEPTH)])
    ssd_par = jnp.stack([_rows8([_pad_lanes(dt_bias[l], LANES), _pad_lanes(a_log[l], LANES)], LANES)
                         for l in range(DEPTH)])
    ssd_par5 = jnp.stack([_rows8([jnp.repeat(d_skip[l], SSM_P), ssm_norm_g[l]], W_GROUP) for l in range(DEPTH)])
    swa_par = jnp.stack([_rows8([jnp.tile(q_norm_g[l], 2), jnp.tile(k_norm_g[l], 2)], LANES) for l in range(DEPTH)])
    sink_par = jnp.broadcast_to(sinks[:, :, None], (DEPTH, ATTN_HEADS, LANES))
    wts = (w_main, w_tail, w_out_b, w_up, w_down, norm1_g.reshape(DEPTH, 1, D_MODEL), norm2_g.reshape(DEPTH, 1, D_MODEL),
           hgrn_par, pool_w_b, pool_scale.reshape(DEPTH, 1, W_GROUP), conv_w, conv_b.reshape(DEPTH, 1, SSM_CONV_DIM),
           ssd_par, ssd_par5, swa_par, sink_par)

    c_all = jnp.concatenate([c_prompt, c_sample], axis=0)
    mod_all = _ada_call(c_all, w_ada, b_ada)
    mod_p = mod_all[:, :bp].reshape(DEPTH, bp * N_MOD, 1, D_MODEL)
    mod_s = mod_all[:, bp:].reshape(DEPTH, bs, N_MOD, D_MODEL)
    mod_s = jnp.repeat(jnp.moveaxis(mod_s, 2, 1), dseq, axis=2)

    grp_p = _make_group(bp, seq, 0, mod_p, 1024, 512)
    grp_s = _make_group(bs, dseq, PAST_LEN, mod_s, bs * dseq, bs * dseq)

    cos_p, sin_p = _rope_tables(jnp.arange(seq))
    cos_s, sin_s = _rope_tables(PAST_LEN + (jnp.arange(TILE_ROWS) % dseq))

    y_p, st_p = _trunk(grp_p, x_prompt.reshape(bp * seq, D_MODEL), mod_p, None, cos_p, sin_p, wts)

    states = (jnp.swapaxes(state_hgrn, -1, -2), state_pool, _ssm_state_to_kernel(state_ssm), state_conv,
              cache_k.reshape(DEPTH, bs, WINDOW, ATTN_KV * ATTN_DH), cache_v.reshape(DEPTH, bs, WINDOW, ATTN_KV * ATTN_DH))
    y_s, st_s = _trunk(grp_s, x_sample.reshape(bs * dseq, D_MODEL), mod_s, states, cos_s, sin_s, wts)

    def finish(st, nb):
        n_h, n_p, n_s, n_c, n_k, n_v = st
        return (jnp.swapaxes(n_h, -1, -2), n_p, _ssm_state_from_kernel(n_s), n_c,
                n_k.reshape(DEPTH, nb, WINDOW, ATTN_KV, ATTN_DH), n_v.reshape(DEPTH, nb, WINDOW, ATTN_KV, ATTN_DH))

    return ((y_p.reshape(bp, seq, D_MODEL), y_s.reshape(bs, dseq, D_MODEL)) + finish(st_p, bp) + finish(st_s, bs))
```

```python
import functools

import jax
import jax.numpy as jnp
from jax import lax
from jax.experimental import pallas as pl
from jax.experimental.pallas import tpu as pltpu

F32 = jnp.float32
BF16 = jnp.bfloat16

D_MODEL = 2048
DEPTH = 4
PAST_LEN = 16384
W_GROUP = 512
HGRN_HEADS = 4
HGRN_DH = 128
POOL_WINDOWS = (2, 4, 8, 16)
POOL_CH = 128
POOL_BUF = 15
SSM_HEADS = 8
SSM_P = 64
SSM_N = 128
SSM_GROUPS = 2
SSM_CONV = 4
SSM_CONV_DIM = 1024
ATTN_HEADS = 8
ATTN_KV = 2
ATTN_DH = 64
WINDOW = 128
ROPE_THETA = 10000.0
D_FF = 4 * D_MODEL
N_MOD = 6
EPS = 1e-6

LANES = 128
SUBLANES = 8
TILE_ROWS = 128
HIST_PAD = 16
VMEM_LIMIT = 56 * 1024 * 1024

COL_AQ, COL_AF, COL_AI, COL_AG = 0, 512, 1024, 1536
COL_PU, COL_CZ, COL_XBC = 2048, 2560, 3072
MAIN_WIDTH = 4096
COL_DQ, COL_DK, COL_DV, COL_DT = 0, 512, 640, 768
TAIL_WIDTH = 896
NEG_BIG = -1e30

_NT = (((1,), (1,)), ((), ()))
_TN = (((0,), (0,)), ((), ()))


def _dot(a, b):
    return jnp.dot(a.astype(BF16), b.astype(BF16), preferred_element_type=F32)


def _dot_nt(a, b):
    return lax.dot_general(a.astype(BF16), b.astype(BF16), _NT, preferred_element_type=F32)


def _dot_tn(a, b):
    return lax.dot_general(a.astype(BF16), b.astype(BF16), _TN, preferred_element_type=F32)


def _sigmoid(x):
    return 1.0 / (1.0 + jnp.exp(-x))


def _cparams(sem):
    return pltpu.CompilerParams(dimension_semantics=sem, vmem_limit_bytes=VMEM_LIMIT)


def _ada_kernel(c_ref, w_ref, b_ref, o_ref):
    c = c_ref[...]
    s = c * _sigmoid(c)
    o_ref[0] = _dot(s, w_ref[0]) + b_ref[0]


def _ada_call(c_all, w_ada, b_ada):
    rows = c_all.shape[0]
    n = w_ada.shape[-1]
    tn = 1024
    return pl.pallas_call(
        _ada_kernel,
        grid=(DEPTH, n // tn),
        in_specs=[
            pl.BlockSpec((rows, D_MODEL), lambda l, j: (0, 0)),
            pl.BlockSpec((1, D_MODEL, tn), lambda l, j: (l, 0, j)),
            pl.BlockSpec((1, 1, tn), lambda l, j: (l, 0, j)),
        ],
        out_specs=pl.BlockSpec((1, rows, tn), lambda l, j: (l, 0, j)),
        out_shape=jax.ShapeDtypeStruct((DEPTH, rows, n), F32),
        compiler_params=_cparams(("arbitrary", "arbitrary")),
        name="ada_mod",
    )(c_all, w_ada, b_ada.reshape(DEPTH, 1, n))


def _modulated_norm(x, g, scale, shift):
    ms = jnp.mean(x * x, axis=-1, keepdims=True)
    y = x * lax.rsqrt(ms + EPS) * g
    return y * (1.0 + scale) + shift


def _layer_spec(l, shape, **kw):
    zeros = (0,) * len(shape)
    return pl.BlockSpec((None,) + tuple(shape), lambda *_: (l,) + zeros, **kw)


def _inproj_kernel(x_ref, g_ref, sc_ref, sh_ref, wm_ref, wt_ref, om_ref, ot_ref, h_ref):
    j = pl.program_id(1)
    n_main = pl.num_programs(1) - 1

    @pl.when(j == 0)
    def _():
        h_ref[...] = _modulated_norm(x_ref[...], g_ref[...], sc_ref[...], sh_ref[...]).astype(BF16)

    @pl.when(j < n_main)
    def _():
        om_ref[...] = jnp.dot(h_ref[...], wm_ref[...], preferred_element_type=F32)

    @pl.when(j == n_main)
    def _():
        ot_ref[...] = jnp.dot(h_ref[...], wt_ref[...], preferred_element_type=F32)


def _inproj_call(grp, l, x, g, mod, w_main, w_tail):
    m, tm = grp["m"], grp["tm"]
    tn = 1024
    n_main = MAIN_WIDTH // tn
    return pl.pallas_call(
        _inproj_kernel,
        grid=(m // tm, n_main + 1),
        in_specs=[
            pl.BlockSpec((tm, D_MODEL), lambda i, j: (i, 0)),
            _layer_spec(l, (1, D_MODEL)),
            grp["mod_spec"](l, 1, tm),
            grp["mod_spec"](l, 0, tm),
            pl.BlockSpec((None, D_MODEL, tn), lambda i, j: (l, 0, jnp.minimum(j, n_main - 1))),
            _layer_spec(l, (D_MODEL, TAIL_WIDTH)),
        ],
        out_specs=[pl.BlockSpec((tm, tn), lambda i, j: (i, jnp.minimum(j, n_main - 1))),
                   pl.BlockSpec((tm, TAIL_WIDTH), lambda i, j: (i, 0))],
        out_shape=[jax.ShapeDtypeStruct((m, MAIN_WIDTH), F32), jax.ShapeDtypeStruct((m, TAIL_WIDTH), F32)],
        scratch_shapes=[pltpu.VMEM((tm, D_MODEL), BF16)],
        compiler_params=_cparams(("arbitrary", "arbitrary")),
        name="in_proj",
    )(x, g, mod, mod, w_main, w_tail)


def _mlp_kernel(x_ref, g_ref, sc_ref, sh_ref, gate_ref, wu_ref, wd_ref, o_ref, h_ref):
    j = pl.program_id(1)

    @pl.when(j == 0)
    def _():
        h_ref[...] = _modulated_norm(x_ref[...], g_ref[...], sc_ref[...], sh_ref[...]).astype(BF16)
        o_ref[...] = jnp.zeros_like(o_ref)

    u = jnp.dot(h_ref[...], wu_ref[...].astype(BF16), preferred_element_type=F32)
    a = jnp.square(jnp.maximum(u, 0.0))
    o_ref[...] += jnp.dot(a.astype(BF16), wd_ref[...].astype(BF16), preferred_element_type=F32)

    @pl.when(j == pl.num_programs(1) - 1)
    def _():
        o_ref[...] = x_ref[...] + gate_ref[...] * o_ref[...]


def _mlp_call(grp, l, x, g, mod, wu, wd):
    m, tm = grp["m"], grp["tm"]
    tf = 512
    return pl.pallas_call(
        _mlp_kernel,
        grid=(m // tm, D_FF // tf),
        in_specs=[
            pl.BlockSpec((tm, D_MODEL), lambda i, j: (i, 0), pipeline_mode=pl.Buffered(1)),
            _layer_spec(l, (1, D_MODEL)),
            grp["mod_spec"](l, 4, tm),
            grp["mod_spec"](l, 3, tm),
            grp["mod_spec"](l, 5, tm),
            pl.BlockSpec((None, D_MODEL, tf), lambda i, j: (l, 0, j)),
            pl.BlockSpec((None, tf, D_MODEL), lambda i, j: (l, j, 0)),
        ],
        out_specs=pl.BlockSpec((tm, D_MODEL), lambda i, j: (i, 0)),
        out_shape=jax.ShapeDtypeStruct((m, D_MODEL), F32),
        scratch_shapes=[pltpu.VMEM((tm, D_MODEL), BF16)],
        compiler_params=_cparams(("arbitrary", "arbitrary")),
        name="mlp",
    )(x, g, mod, mod, mod, wu, wd)


def _row_iota(shape):
    return lax.broadcasted_iota(jnp.int32, shape, 0)


def _lane_iota(shape):
    return lax.broadcasted_iota(jnp.int32, shape, 1)


def _cat(pieces, axis):
    return pieces[0] if len(pieces) == 1 else jnp.concatenate(pieces, axis=axis)


def _cumsum_rows(x, lseq):
    pos = _row_iota(x.shape) & (lseq - 1)
    s = 1
    while s < lseq:
        x = x + jnp.where(pos >= s, pltpu.roll(x, s, axis=0), 0.0)
        s *= 2
    return x


def _seq_last_rows(x, nseq, lseq):
    return _cat([jnp.broadcast_to(x[(b + 1) * lseq - 1:(b + 1) * lseq, :], (lseq, x.shape[1]))
                 for b in range(nseq)], 0)


def _hgrn_body(pm_ref, par_ref, st_ref, *, nseq, lseq):
    rows = TILE_ROWS
    dh = HGRN_DH
    aq = pm_ref[:, COL_AQ:COL_AQ + W_GROUP]
    xf = pm_ref[:, COL_AF:COL_AF + W_GROUP]
    v = pm_ref[:, COL_AI:COL_AI + W_GROUP]
    q = aq * _sigmoid(aq)
    e = jnp.exp(-jnp.abs(xf))
    inv = 1.0 / (1.0 + e)
    log_sig = jnp.minimum(xf, 0.0) - jnp.log1p(e)
    log_lb = par_ref[0:1, :]
    bterm = par_ref[1:2, :] + log_sig
    lf = jnp.maximum(log_lb, bterm) + jnp.log1p(jnp.exp(-jnp.abs(log_lb - bterm)))
    one_m_lb = par_ref[2:3, :]
    k = one_m_lb * jnp.where(xf >= 0.0, e * inv, inv)
    f = par_ref[4:5, :] + one_m_lb * jnp.where(xf >= 0.0, inv, e * inv)
    cum = _cumsum_rows(lf, lseq)

    row = _row_iota((rows, W_GROUP))
    heads = [slice(hd * dh, (hd + 1) * dh) for hd in range(HGRN_HEADS)]
    o = [jnp.zeros((rows, dh), F32) for _ in heads]

    sub = min(SUBLANES, lseq)
    fm = jnp.where((row & (sub - 1)) == 0, 0.0, f)
    kg = k
    vs = v
    for d in range(sub):
        if d > 0:
            kg = fm * pltpu.roll(kg, 1, axis=0)
            vs = pltpu.roll(vs, 1, axis=0)
        term = q * kg
        for hd, sl in enumerate(heads):
            o[hd] = o[hd] + jnp.sum(term[:, sl], axis=-1, keepdims=True) * vs[:, sl]

    levels = []
    h = lseq // 2
    while h >= sub:
        levels.append(h)
        h //= 2
    if levels:
        rr = _row_iota((rows, rows))
        cc = _lane_iota((rows, rows))
        p = [jnp.zeros((rows, rows), F32) for _ in heads]
        for h in levels:
            upper = (row & h) != 0
            refm = _cat([jnp.broadcast_to(cum[jb * 2 * h + h - 1:jb * 2 * h + h, :], (2 * h, W_GROUP))
                         for jb in range(rows // (2 * h))], 0)
            x = jnp.exp(jnp.where(upper, cum - refm, refm - cum))
            a_side = jnp.where(upper, q * x, 0.0).astype(BF16)
            b_side = jnp.where(upper, 0.0, k * x).astype(BF16)
            if 2 * h < rows:
                same = (rr // (2 * h)) == (cc // (2 * h))
            for hd, sl in enumerate(heads):
                s = lax.dot_general(a_side[:, sl], b_side[:, sl], _NT, preferred_element_type=F32)
                if 2 * h < rows:
                    s = jnp.where(same, s, 0.0)
                p[hd] = p[hd] + s
        for hd, sl in enumerate(heads):
            o[hd] = o[hd] + _dot(p[hd], v[:, sl])

    qe = (q * jnp.exp(cum)).astype(BF16)
    lastm = _seq_last_rows(cum, nseq, lseq)
    kd = k * jnp.exp(lastm - cum)
    dec = jnp.exp(lastm)
    vb = v.astype(BF16)
    seq_of_row = row // lseq
    for hd, sl in enumerate(heads):
        inter = []
        for b in range(nseq):
            rs = slice(b * lseq, (b + 1) * lseq)
            st = st_ref[b, hd]
            inter.append(lax.dot_general(qe[rs, sl], st.astype(BF16), _NT, preferred_element_type=F32))
            kdb = kd if nseq == 1 else jnp.where(seq_of_row == b, kd, 0.0)
            upd = lax.dot_general(vb[:, sl], kdb[:, sl].astype(BF16), _TN, preferred_element_type=F32)
            st_ref[b, hd] = st * dec[b * lseq:b * lseq + 1, sl] + upd
        o[hd] = o[hd] + _cat(inter, 0)

    gate = _sigmoid(pm_ref[:, COL_AG:COL_AG + W_GROUP])
    out = []
    for hd, sl in enumerate(heads):
        ms = jnp.mean(o[hd] * o[hd], axis=-1, keepdims=True)
        out.append(o[hd] * lax.rsqrt(ms + EPS) * par_ref[3:4, sl] * gate[:, sl])
    return out


def _pool_body(pm_ref, pw_ref, sc_ref, hout_ref, ext, i, *, nseq, lseq, pos0):
    stride = lseq + HIST_PAD
    u = pm_ref[:, COL_PU:COL_PU + W_GROUP]
    for b in range(nseq):
        base = b * stride
        ext[base + HIST_PAD - POOL_BUF:base + HIST_PAD, :] = hout_ref[b]
        ext[base + HIST_PAD:base + HIST_PAD + lseq, :] = u[b * lseq:(b + 1) * lseq, :]

    local = _row_iota((TILE_ROWS, POOL_CH)) & (lseq - 1)
    posn = pos0 + i * lseq + local
    out = []
    for gi, win in enumerate(POOL_WINDOWS):
        cs = slice(gi * POOL_CH, (gi + 1) * POOL_CH)
        pieces = []
        for b in range(nseq):
            base = b * stride + HIST_PAD
            s = ext[base:base + lseq, cs]
            for j in range(1, win):
                s = s + ext[pl.ds(base - j, lseq), cs]
            pieces.append(s)
        cnt = jnp.minimum(posn + 1, win).astype(F32)
        pooled = _cat(pieces, 0) / cnt - u[:, cs]
        out.append(_dot(pooled, pw_ref[gi]) * sc_ref[:, cs])

    for b in range(nseq):
        base = b * stride
        hout_ref[b] = ext[base + lseq + HIST_PAD - POOL_BUF:base + lseq + HIST_PAD, :]
    return out


def _expand_heads(z, emat):
    hi = z.astype(BF16)
    lo = (z - hi.astype(F32)).astype(BF16)
    return (jnp.dot(hi, emat, preferred_element_type=F32) + jnp.dot(lo, emat, preferred_element_type=F32))


def _ssd_body(pm_ref, pt_ref, cw_ref, cb_ref, par_ref, par5_ref, cout_ref, st_ref, ext, *, nseq, lseq):
    rows = TILE_ROWS
    hist = SSM_CONV - 1
    stride = lseq + SUBLANES
    gw = SSM_N

    xbc = pm_ref[:, COL_XBC:COL_XBC + SSM_CONV_DIM]
    for b in range(nseq):
        base = b * stride
        ext[base + SUBLANES - hist:base + SUBLANES, :] = cout_ref[b]
        ext[base + SUBLANES:base + SUBLANES + lseq, :] = xbc[b * lseq:(b + 1) * lseq, :]
    acc = None
    for j in range(SSM_CONV):
        sh = _cat([ext[pl.ds(b * stride + SUBLANES - hist + j, lseq), :] for b in range(nseq)], 0)
        t = sh * cw_ref[j:j + 1, :]
        acc = t if acc is None else acc + t
    for b in range(nseq):
        base = b * stride
        cout_ref[b] = ext[base + lseq + SUBLANES - hist:base + lseq + SUBLANES, :]
    acc = acc + cb_ref[...]
    conv = acc * _sigmoid(acc)
    x = conv[:, 0:W_GROUP]
    bm = conv[:, W_GROUP:W_GROUP + SSM_GROUPS * gw]
    cm = conv[:, W_GROUP + SSM_GROUPS * gw:]

    pre = pt_ref[:, COL_DT:COL_DT + LANES] + par_ref[0:1, :]
    dt = jnp.maximum(pre, 0.0) + jnp.log1p(jnp.exp(-jnp.abs(pre)))
    a = dt * (-jnp.exp(par_ref[1:2, :]))
    cum = _cumsum_rows(a, lseq)
    cum_t = cum.T
    dt_t = dt.T

    rr = _row_iota((rows, rows))
    cc = _lane_iota((rows, rows))
    valid = cc <= rr
    if nseq > 1:
        valid = jnp.logical_and(valid, (rr // lseq) == (cc // lseq))

    erow = _row_iota((LANES, W_GROUP))
    ecol = _lane_iota((LANES, W_GROUP))
    emat = jnp.where(ecol // SSM_P == erow, 1.0, 0.0).astype(BF16)

    lane = _lane_iota((rows, LANES))
    lo_half = lane < SSM_P
    heads_per_group = SSM_HEADS // SSM_GROUPS
    y_chunks = []
    for g in range(SSM_GROUPS):
        gs = slice(g * gw, (g + 1) * gw)
        cb = lax.dot_general(cm[:, gs].astype(BF16), bm[:, gs].astype(BF16), _NT, preferred_element_type=F32)
        for jc in range(heads_per_group // 2):
            chunk = g * (heads_per_group // 2) + jc
            xc = x[:, chunk * LANES:(chunk + 1) * LANES]
            yc = None
            for half in range(2):
                r = 2 * chunk + half
                seg = jnp.broadcast_to(cum[:, r:r + 1], (rows, rows)) - cum_t[r:r + 1, :]
                wts = cb * jnp.exp(jnp.where(valid, seg, NEG_BIG)) * dt_t[r:r + 1, :]
                xm = jnp.where(lo_half, xc, 0.0) if half == 0 else jnp.where(lo_half, 0.0, xc)
                t = _dot(wts, xm)
                yc = t if yc is None else yc + t
            y_chunks.append(yc)
    y = jnp.concatenate(y_chunks, axis=1)

    lastm = _seq_last_rows(cum, nseq, lseq)
    e_cum = _expand_heads(jnp.exp(cum), emat)
    e_wst = _expand_heads(dt * jnp.exp(lastm - cum), emat)
    e_dec = _expand_heads(jnp.exp(lastm), emat)
    xs = x * e_wst
    row5 = _row_iota((rows, W_GROUP))
    inter_groups = []
    pw = heads_per_group * SSM_P
    for g in range(SSM_GROUPS):
        gs = slice(g * gw, (g + 1) * gw)
        ps = slice(g * pw, (g + 1) * pw)
        inter = []
        for b in range(nseq):
            rs = slice(b * lseq, (b + 1) * lseq)
            st = st_ref[b, g]
            inter.append(_dot(cm[rs, gs], st))
            xsb = xs if nseq == 1 else jnp.where(row5 // lseq == b, xs, 0.0)
            upd = _dot_tn(bm[:, gs], xsb[:, ps])
            st_ref[b, g] = st * e_dec[b * lseq:b * lseq + 1, ps] + upd
        inter_groups.append(_cat(inter, 0))
    y = y + jnp.concatenate(inter_groups, axis=1) * e_cum
    y = y + x * par5_ref[0:1, :]
    z = pm_ref[:, COL_CZ:COL_CZ + W_GROUP]
    y = y * (z * _sigmoid(z))
    ms = jnp.mean(y * y, axis=-1, keepdims=True)
    return y * lax.rsqrt(ms + EPS) * par5_ref[1:2, :]


def _head_rmsnorm(x, g, lo_half):
    sq = x * x
    s_lo = jnp.sum(jnp.where(lo_half, sq, 0.0), axis=-1, keepdims=True)
    s_hi = jnp.sum(jnp.where(lo_half, 0.0, sq), axis=-1, keepdims=True)
    ms = jnp.where(lo_half, s_lo, s_hi) * (1.0 / ATTN_DH)
    return x * lax.rsqrt(ms + EPS) * g


def _rope(x, cos, sin_signed, upper_half):
    partner = jnp.where(upper_half, pltpu.roll(x, ATTN_DH // 2, axis=1), pltpu.roll(x, LANES - ATTN_DH // 2, axis=1))
    return x * cos + partner * sin_signed


def _dup_head(x, g, lane):
    own = jnp.where((lane // ATTN_DH) == g, x, 0.0)
    return own + pltpu.roll(own, ATTN_DH, axis=1)


def _swa_body(pt_ref, cos_ref, sin_ref, par_ref, sink_ref, kout_ref, vout_ref, i, *, nseq, lseq, has_state):
    rows = TILE_ROWS
    rep = ATTN_HEADS // ATTN_KV
    lane = _lane_iota((rows, LANES))
    lo_half = lane < ATTN_DH
    upper_half = (lane & (ATTN_DH // 2)) != 0
    cos = cos_ref[...]
    sin = sin_ref[...]
    kn = _rope(_head_rmsnorm(pt_ref[:, COL_DK:COL_DK + LANES], par_ref[1:2, :], lo_half), cos, sin, upper_half)
    vn = pt_ref[:, COL_DV:COL_DV + LANES]
    scale = ATTN_DH ** -0.5
    qs = []
    for c in range(ATTN_HEADS // 2):
        qc = pt_ref[:, COL_DQ + c * LANES:COL_DQ + (c + 1) * LANES]
        qs.append(_rope(_head_rmsnorm(qc, par_ref[0:1, :], lo_half), cos, sin, upper_half) * scale)

    mq = rep * lseq
    qrow = _row_iota((mq, LANES))
    q_local = qrow & (lseq - 1)
    kcol = _lane_iota((mq, LANES))
    hist_ok = jnp.logical_or(has_state, i > 0)
    hist_valid = jnp.logical_and(kcol > q_local, hist_ok)
    lane_w = _lane_iota((WINDOW, LANES))

    out_rows = [[None] * nseq for _ in range(ATTN_HEADS // 2)]
    for g in range(ATTN_KV):
        k_new = _dup_head(kn, g, lane).astype(BF16)
        v_new = _dup_head(vn, g, lane).astype(BF16)
        for b in range(nseq):
            rs = slice(b * lseq, (b + 1) * lseq)
            q4 = []
            sink_rows = []
            for r in range(rep):
                hidx = g * rep + r
                qc = qs[hidx // 2][rs, :]
                lo_l = lo_half[0:lseq, :]
                q4.append(jnp.where(lo_l, qc, 0.0) if hidx % 2 == 0 else jnp.where(lo_l, 0.0, qc))
                sink_rows.append(jnp.broadcast_to(sink_ref[hidx:hidx + 1, 0:1], (lseq, 1)))
            q4 = jnp.concatenate(q4, axis=0).astype(BF16)
            sink = jnp.concatenate(sink_rows, axis=0)
            k_hist = _dup_head(kout_ref[b], g, lane_w).astype(BF16)
            v_hist = _dup_head(vout_ref[b], g, lane_w).astype(BF16)
            s_h = lax.dot_general(q4, k_hist, _NT, preferred_element_type=F32)
            s_n = lax.dot_general(q4, k_new, _NT, preferred_element_type=F32)
            s_h = jnp.where(hist_valid, s_h, NEG_BIG)
            new_valid = jnp.logical_and(kcol // lseq == b, (kcol & (lseq - 1)) <= q_local)
            s_n = jnp.where(new_valid, s_n, NEG_BIG)
            m = jnp.maximum(jnp.maximum(jnp.max(s_h, axis=-1, keepdims=True),
                                        jnp.max(s_n, axis=-1, keepdims=True)), sink)
            p_h = jnp.exp(s_h - m)
            p_n = jnp.exp(s_n - m)
            den = (jnp.sum(p_h, axis=-1, keepdims=True) + jnp.sum(p_n, axis=-1, keepdims=True)
                   + jnp.exp(sink - m))
            o4 = (jnp.dot(p_h.astype(BF16), v_hist, preferred_element_type=F32)
                  + jnp.dot(p_n.astype(BF16), v_new, preferred_element_type=F32)) / den
            for pair in range(rep // 2):
                c = (g * rep) // 2 + pair
                o_lo = o4[(2 * pair) * lseq:(2 * pair + 1) * lseq, :]
                o_hi = o4[(2 * pair + 1) * lseq:(2 * pair + 2) * lseq, :]
                out_rows[c][b] = jnp.where(lo_half[0:lseq, :], o_lo, o_hi)

    for b in range(nseq):
        rs = slice(b * lseq, (b + 1) * lseq)
        if lseq < WINDOW:
            keep_k = kout_ref[b, lseq:WINDOW, :]
            keep_v = vout_ref[b, lseq:WINDOW, :]
            kout_ref[b, 0:WINDOW - lseq, :] = keep_k
            vout_ref[b, 0:WINDOW - lseq, :] = keep_v
        kout_ref[b, WINDOW - lseq:WINDOW, :] = kn[rs, :]
        vout_ref[b, WINDOW - lseq:WINDOW, :] = vn[rs, :]
    return [_cat(out_rows[c], 0) for c in range(ATTN_HEADS // 2)]


N_STATES = 6


def _mix_kernel(*refs, nseq, lseq, has_state, pos0):
    (pm_ref, pt_ref, x_ref, gate_ref, w_ref, cos_ref, sin_ref, hpar_ref, pw_ref, psc_ref, cw_ref, cb_ref,
     spar_ref, spar5_ref, apar_ref, sink_ref) = refs[:16]
    rest = refs[16:]
    if has_state:
        init_refs, rest = rest[:N_STATES], rest[N_STATES:]
    else:
        init_refs = (None,) * N_STATES
    o_ref = rest[0]
    state_refs = rest[1:1 + N_STATES]
    pool_ext, conv_ext = rest[1 + N_STATES:]
    nh_ref, np_ref, nc_ref, ns_ref, nk_ref, nv_ref = state_refs
    i = pl.program_id(1)

    @pl.when(i == 0)
    def _():
        for dst, src in zip(state_refs, init_refs):
            dst[...] = jnp.zeros_like(dst) if src is None else src[...]

    ya = _hgrn_body(pm_ref, hpar_ref, nh_ref, nseq=nseq, lseq=lseq)
    yb = _pool_body(pm_ref, pw_ref, psc_ref, np_ref, pool_ext, i, nseq=nseq, lseq=lseq, pos0=pos0)
    yc = _ssd_body(pm_ref, pt_ref, cw_ref, cb_ref, spar_ref, spar5_ref, nc_ref, ns_ref, conv_ext, nseq=nseq, lseq=lseq)
    yd = _swa_body(pt_ref, cos_ref, sin_ref, apar_ref, sink_ref, nk_ref, nv_ref, i, nseq=nseq, lseq=lseq,
                   has_state=has_state)
    y = jnp.concatenate([t.astype(BF16) for t in ya + yb + [yc] + yd], axis=1)
    mixed = jnp.dot(y, w_ref[...], preferred_element_type=F32)
    o_ref[...] = x_ref[...] + gate_ref[...] * mixed


def _mix_call(grp, l, x, proj, tail, mod, w_out, cos, sin, pars, states):
    nseq, lseq, no, nt = grp["nseq"], grp["lseq"], grp["no"], grp["nt"]
    has_state = states is not None
    hpar, pool_w, pool_sc, conv_w, conv_b, spar, spar5, apar, sink = pars
    tails = [(HGRN_HEADS, HGRN_DH, HGRN_DH), (POOL_BUF, W_GROUP), (SSM_CONV - 1, SSM_CONV_DIM),
             (SSM_GROUPS, SSM_N, (SSM_HEADS // SSM_GROUPS) * SSM_P),
             (WINDOW, ATTN_KV * ATTN_DH), (WINDOW, ATTN_KV * ATTN_DH)]

    def row_spec(width):
        return pl.BlockSpec((TILE_ROWS, width), lambda o, i: (o * nt + i, 0))

    in_specs = [
        row_spec(MAIN_WIDTH), row_spec(TAIL_WIDTH), row_spec(D_MODEL),
        grp["mod_spec"](l, 2, TILE_ROWS, lambda o, i: o * nt + i),
        _layer_spec(l, (D_MODEL, D_MODEL), pipeline_mode=pl.Buffered(1)),
        pl.BlockSpec((TILE_ROWS, LANES), lambda o, i: (i, 0)),
        pl.BlockSpec((TILE_ROWS, LANES), lambda o, i: (i, 0)),
        _layer_spec(l, (8, W_GROUP)), _layer_spec(l, (len(POOL_WINDOWS), POOL_CH, POOL_CH)),
        _layer_spec(l, (1, W_GROUP)), _layer_spec(l, (SSM_CONV, SSM_CONV_DIM)), _layer_spec(l, (1, SSM_CONV_DIM)),
        _layer_spec(l, (8, LANES)), _layer_spec(l, (8, W_GROUP)), _layer_spec(l, (8, LANES)), _layer_spec(l, (8, LANES)),
    ]
    args = [proj, tail, x, mod, w_out, cos, sin, hpar, pool_w, pool_sc, conv_w, conv_b, spar, spar5, apar, sink]
    if has_state:
        for t in tails:
            zeros = (0,) * len(t)
            in_specs.append(pl.BlockSpec((None, nseq) + t, lambda o, i, zeros=zeros: (l, o) + zeros,
                                         pipeline_mode=pl.Buffered(1)))
        args += list(states)
    out_specs = [row_spec(D_MODEL)]
    out_shape = [jax.ShapeDtypeStruct((grp["m"], D_MODEL), F32)]
    state_mode = dict(pipeline_mode=pl.Buffered(1)) if nseq > 1 else {}
    for t in tails:
        zeros = (0,) * len(t)
        out_specs.append(pl.BlockSpec((nseq,) + t, lambda o, i, zeros=zeros: (o,) + zeros, **state_mode))
        out_shape.append(jax.ShapeDtypeStruct((grp["nb"],) + t, F32))
    return pl.pallas_call(
        functools.partial(_mix_kernel, nseq=nseq, lseq=lseq, has_state=has_state, pos0=grp["pos0"]),
        grid=(no, nt),
        in_specs=in_specs,
        out_specs=out_specs,
        out_shape=out_shape,
        scratch_shapes=[pltpu.VMEM((nseq * (lseq + HIST_PAD), W_GROUP), F32),
                        pltpu.VMEM((nseq * (lseq + SUBLANES), SSM_CONV_DIM), F32)],
        compiler_params=_cparams(("arbitrary", "arbitrary")),
        name="mix",
    )(*args)


def _rope_tables(pos):
    half = ATTN_DH // 2
    inv = ROPE_THETA ** (-jnp.arange(half, dtype=F32) / half)
    ang = pos.astype(F32)[:, None] * inv[None]
    cos = jnp.tile(jnp.cos(ang), (1, LANES // half))
    sin = jnp.sin(ang)
    sin_signed = jnp.tile(jnp.concatenate([-sin, sin], axis=1), (1, LANES // ATTN_DH))
    return cos, sin_signed


def _pad_lanes(v, width):
    return jnp.pad(v, (0, width - v.shape[0]))


def _rows8(rows_list, width):
    out = jnp.zeros((8, width), F32)
    for r, v in enumerate(rows_list):
        out = out.at[r].set(v)
    return out


def _make_group(nb, seq_len, pos0, mod_rows, tm):
    m = nb * seq_len
    if seq_len >= TILE_ROWS:
        nseq, lseq = 1, TILE_ROWS
        no, nt = nb, seq_len // TILE_ROWS
    else:
        nseq, lseq = TILE_ROWS // seq_len, seq_len
        no, nt = m // TILE_ROWS, 1
    per_row = mod_rows.shape[-2] != 1

    def mod_spec(l, kind, tile, row_tile=lambda i, *_: i):
        if per_row:
            def imap(*idx):
                return (l, kind, row_tile(*idx), 0)
            return pl.BlockSpec((None, None, tile, D_MODEL), imap)

        def imap(*idx):
            return (l, (row_tile(*idx) * tile) // seq_len * N_MOD + kind, 0, 0)
        return pl.BlockSpec((None, None, 1, D_MODEL), imap)

    return dict(nb=nb, m=m, tm=tm, nseq=nseq, lseq=lseq, no=no, nt=nt, pos0=pos0, mod_spec=mod_spec)


def _trunk(grp, x, mod, states, cos, sin, wts):
    w_main, w_tail, w_out_b, w_up, w_down, norm1_g, norm2_g, mix_pars = wts
    outs = [[] for _ in range(N_STATES)]
    for l in range(DEPTH):
        proj, tail = _inproj_call(grp, l, x, norm1_g, mod, w_main, w_tail)
        x, *new_states = _mix_call(grp, l, x, proj, tail, mod, w_out_b, cos, sin, mix_pars, states)
        x = _mlp_call(grp, l, x, norm2_g, mod, w_up, w_down)
        for lst, val in zip(outs, new_states):
            lst.append(val)
    return x, [jnp.stack(o) for o in outs]


def _ssm_state_to_kernel(s):
    lead = s.shape[:-3]
    r = SSM_HEADS // SSM_GROUPS
    s = s.reshape(lead + (SSM_GROUPS, r, SSM_P, SSM_N))
    s = jnp.moveaxis(s, -1, -3)
    return s.reshape(lead + (SSM_GROUPS, SSM_N, r * SSM_P))


def _ssm_state_from_kernel(s):
    lead = s.shape[:-3]
    r = SSM_HEADS // SSM_GROUPS
    s = s.reshape(lead + (SSM_GROUPS, SSM_N, r, SSM_P))
    s = jnp.moveaxis(s, -3, -1)
    return s.reshape(lead + (SSM_HEADS, SSM_P, SSM_N))


def kernel(x_prompt, x_sample, c_prompt, c_sample, state_hgrn, state_pool, state_ssm, state_conv, cache_k, cache_v, norm1_g, norm2_g, w_ada, b_ada, w_in, hgrn_lb_logits, hgrn_norm_g, pool_w, pool_scale, conv_w, conv_b, dt_bias, a_log, d_skip, ssm_norm_g, q_norm_g, k_norm_g, sinks, w_out, w_up, w_down):
    bp, seq, _ = x_prompt.shape
    bs, dseq, _ = x_sample.shape

    w_main = w_in[:, :, :MAIN_WIDTH].astype(BF16)
    dt_end = MAIN_WIDTH + SSM_HEADS
    w_tail = jnp.concatenate(
        [w_in[:, :, dt_end:], w_in[:, :, MAIN_WIDTH:dt_end],
         jnp.zeros((DEPTH, D_MODEL, LANES - SSM_HEADS), F32)], axis=-1).astype(BF16)
    w_out_b = w_out.astype(BF16)
    pool_w_b = pool_w.astype(BF16)

    p = jax.nn.softmax(hgrn_lb_logits.astype(F32), axis=0)
    cs = jnp.cumsum(p, axis=0)
    lbs = cs - cs[:1]
    hgrn_par = jnp.stack([_rows8([jnp.log(lbs[l]), jnp.log1p(-lbs[l]), 1.0 - lbs[l], hgrn_norm_g[l], lbs[l]], W_GROUP)
                          for l in range(DEPTH)])
    ssd_par = jnp.stack([_rows8([_pad_lanes(dt_bias[l], LANES), _pad_lanes(a_log[l], LANES)], LANES)
                         for l in range(DEPTH)])
    ssd_par5 = jnp.stack([_rows8([jnp.repeat(d_skip[l], SSM_P), ssm_norm_g[l]], W_GROUP) for l in range(DEPTH)])
    swa_par = jnp.stack([_rows8([jnp.tile(q_norm_g[l], 2), jnp.tile(k_norm_g[l], 2)], LANES) for l in range(DEPTH)])
    sink_par = jnp.broadcast_to(sinks[:, :, None], (DEPTH, ATTN_HEADS, LANES))
    mix_pars = (hgrn_par, pool_w_b, pool_scale.reshape(DEPTH, 1, W_GROUP), conv_w,
                conv_b.reshape(DEPTH, 1, SSM_CONV_DIM), ssd_par, ssd_par5, swa_par, sink_par)
    wts = (w_main, w_tail, w_out_b, w_up, w_down, norm1_g.reshape(DEPTH, 1, D_MODEL),
           norm2_g.reshape(DEPTH, 1, D_MODEL), mix_pars)

    c_all = jnp.concatenate([c_prompt, c_sample], axis=0)
    mod_all = _ada_call(c_all, w_ada, b_ada)
    mod_p = mod_all[:, :bp].reshape(DEPTH, bp * N_MOD, 1, D_MODEL)
    mod_s = mod_all[:, bp:].reshape(DEPTH, bs, N_MOD, D_MODEL)
    mod_s = jnp.repeat(jnp.moveaxis(mod_s, 2, 1), dseq, axis=2)

    grp_p = _make_group(bp, seq, 0, mod_p, 1024)
    grp_s = _make_group(bs, dseq, PAST_LEN, mod_s, bs * dseq)

    cos_p, sin_p = _rope_tables(jnp.arange(seq))
    cos_s, sin_s = _rope_tables(PAST_LEN + (jnp.arange(TILE_ROWS) % dseq))

    y_p, st_p = _trunk(grp_p, x_prompt.reshape(bp * seq, D_MODEL), mod_p, None, cos_p, sin_p, wts)

    kv_flat = (DEPTH, bs, WINDOW, ATTN_KV * ATTN_DH)
    states = (jnp.swapaxes(state_hgrn, -1, -2), state_pool, state_conv, _ssm_state_to_kernel(state_ssm),
              cache_k.reshape(kv_flat), cache_v.reshape(kv_flat))
    y_s, st_s = _trunk(grp_s, x_sample.reshape(bs * dseq, D_MODEL), mod_s, states, cos_s, sin_s, wts)

    def finish(st, nb):
        n_h, n_p, n_c, n_s, n_k, n_v = st
        return (jnp.swapaxes(n_h, -1, -2), n_p, _ssm_state_from_kernel(n_s), n_c,
                n_k.reshape(DEPTH, nb, WINDOW, ATTN_KV, ATTN_DH), n_v.reshape(DEPTH, nb, WINDOW, ATTN_KV, ATTN_DH))

    return ((y_p.reshape(bp, seq, D_MODEL), y_s.reshape(bs, dseq, D_MODEL)) + finish(st_p, bp) + finish(st_s, bs))
```

```python
import functools

import jax
import jax.numpy as jnp
from jax import lax
from jax.experimental import pallas as pl
from jax.experimental.pallas import tpu as pltpu

F32 = jnp.float32
BF16 = jnp.bfloat16

D_MODEL = 2048
DEPTH = 4
PAST_LEN = 16384
W_GROUP = 512
HGRN_HEADS = 4
HGRN_DH = 128
POOL_WINDOWS = (2, 4, 8, 16)
POOL_CH = 128
POOL_BUF = 15
SSM_HEADS = 8
SSM_P = 64
SSM_N = 128
SSM_GROUPS = 2
SSM_CONV = 4
SSM_CONV_DIM = 1024
ATTN_HEADS = 8
ATTN_KV = 2
ATTN_DH = 64
WINDOW = 128
ROPE_THETA = 10000.0
D_FF = 4 * D_MODEL
N_MOD = 6
EPS = 1e-6

LANES = 128
SUBLANES = 8
TILE_ROWS = 128
HIST_PAD = 16
VMEM_LIMIT = 56 * 1024 * 1024

COL_AQ, COL_AF, COL_AI, COL_AG = 0, 512, 1024, 1536
COL_PU, COL_CZ, COL_XBC = 2048, 2560, 3072
MAIN_WIDTH = 4096
COL_DQ, COL_DK, COL_DV, COL_DT = 0, 512, 640, 768
TAIL_WIDTH = 896
NEG_BIG = -1e30

_NT = (((1,), (1,)), ((), ()))
_TN = (((0,), (0,)), ((), ()))


def _dot(a, b):
    return jnp.dot(a.astype(BF16), b.astype(BF16), preferred_element_type=F32)


def _dot_nt(a, b):
    return lax.dot_general(a.astype(BF16), b.astype(BF16), _NT, preferred_element_type=F32)


def _dot_tn(a, b):
    return lax.dot_general(a.astype(BF16), b.astype(BF16), _TN, preferred_element_type=F32)


def _sigmoid(x):
    return 1.0 / (1.0 + jnp.exp(-x))


def _cparams(sem):
    return pltpu.CompilerParams(dimension_semantics=sem, vmem_limit_bytes=VMEM_LIMIT)


def _ada_kernel(c_ref, w_ref, b_ref, o_ref):
    c = c_ref[...]
    s = c * _sigmoid(c)
    o_ref[0] = _dot(s, w_ref[0]) + b_ref[0]


def _ada_call(c_all, w_ada, b_ada):
    rows = c_all.shape[0]
    n = w_ada.shape[-1]
    tn = 1024
    return pl.pallas_call(
        _ada_kernel,
        grid=(DEPTH, n // tn),
        in_specs=[
            pl.BlockSpec((rows, D_MODEL), lambda l, j: (0, 0)),
            pl.BlockSpec((1, D_MODEL, tn), lambda l, j: (l, 0, j)),
            pl.BlockSpec((1, 1, tn), lambda l, j: (l, 0, j)),
        ],
        out_specs=pl.BlockSpec((1, rows, tn), lambda l, j: (l, 0, j)),
        out_shape=jax.ShapeDtypeStruct((DEPTH, rows, n), F32),
        compiler_params=_cparams(("arbitrary", "arbitrary")),
        name="ada_mod",
    )(c_all, w_ada, b_ada.reshape(DEPTH, 1, n))


def _modulated_norm(x, g, scale, shift):
    ms = jnp.mean(x * x, axis=-1, keepdims=True)
    y = x * lax.rsqrt(ms + EPS) * g
    return y * (1.0 + scale) + shift


def _layer_spec(l, shape, **kw):
    zeros = (0,) * len(shape)
    return pl.BlockSpec((None,) + tuple(shape), lambda *_: (l,) + zeros, **kw)


def _inproj_kernel(x_ref, g_ref, sc_ref, sh_ref, wm_ref, wt_ref, om_ref, ot_ref, h_ref):
    j = pl.program_id(1)
    n_main = pl.num_programs(1) - 1

    @pl.when(j == 0)
    def _():
        h_ref[...] = _modulated_norm(x_ref[...], g_ref[...], sc_ref[...], sh_ref[...]).astype(BF16)

    @pl.when(j < n_main)
    def _():
        om_ref[...] = jnp.dot(h_ref[...], wm_ref[...], preferred_element_type=F32)

    @pl.when(j == n_main)
    def _():
        ot_ref[...] = jnp.dot(h_ref[...], wt_ref[...], preferred_element_type=F32)


def _inproj_call(grp, l, x, g, mod, w_main, w_tail):
    m, tm = grp["m"], grp["tm"]
    tn = 1024
    n_main = MAIN_WIDTH // tn
    return pl.pallas_call(
        _inproj_kernel,
        grid=(m // tm, n_main + 1),
        in_specs=[
            pl.BlockSpec((tm, D_MODEL), lambda i, j: (i, 0)),
            _layer_spec(l, (1, D_MODEL)),
            grp["mod_spec"](l, 1, tm),
            grp["mod_spec"](l, 0, tm),
            pl.BlockSpec((None, D_MODEL, tn), lambda i, j: (l, 0, jnp.minimum(j, n_main - 1))),
            _layer_spec(l, (D_MODEL, TAIL_WIDTH)),
        ],
        out_specs=[pl.BlockSpec((tm, tn), lambda i, j: (i, jnp.minimum(j, n_main - 1))),
                   pl.BlockSpec((tm, TAIL_WIDTH), lambda i, j: (i, 0))],
        out_shape=[jax.ShapeDtypeStruct((m, MAIN_WIDTH), F32), jax.ShapeDtypeStruct((m, TAIL_WIDTH), F32)],
        scratch_shapes=[pltpu.VMEM((tm, D_MODEL), BF16)],
        compiler_params=_cparams(("arbitrary", "arbitrary")),
        name="in_proj",
    )(x, g, mod, mod, w_main, w_tail)


def _mlp_kernel(x_ref, g_ref, sc_ref, sh_ref, gate_ref, wu_ref, wd_ref, o_ref, h_ref):
    j = pl.program_id(1)

    @pl.when(j == 0)
    def _():
        h_ref[...] = _modulated_norm(x_ref[...], g_ref[...], sc_ref[...], sh_ref[...]).astype(BF16)
        o_ref[...] = jnp.zeros_like(o_ref)

    u = jnp.dot(h_ref[...], wu_ref[...].astype(BF16), preferred_element_type=F32)
    a = jnp.square(jnp.maximum(u, 0.0))
    o_ref[...] += jnp.dot(a.astype(BF16), wd_ref[...].astype(BF16), preferred_element_type=F32)

    @pl.when(j == pl.num_programs(1) - 1)
    def _():
        o_ref[...] = x_ref[...] + gate_ref[...] * o_ref[...]


def _mlp_call(grp, l, x, g, mod, wu, wd):
    m, tm = grp["m"], grp["tm"]
    tf = 512
    return pl.pallas_call(
        _mlp_kernel,
        grid=(m // tm, D_FF // tf),
        in_specs=[
            pl.BlockSpec((tm, D_MODEL), lambda i, j: (i, 0), pipeline_mode=pl.Buffered(1)),
            _layer_spec(l, (1, D_MODEL)),
            grp["mod_spec"](l, 4, tm),
            grp["mod_spec"](l, 3, tm),
            grp["mod_spec"](l, 5, tm),
            pl.BlockSpec((None, D_MODEL, tf), lambda i, j: (l, 0, j)),
            pl.BlockSpec((None, tf, D_MODEL), lambda i, j: (l, j, 0)),
        ],
        out_specs=pl.BlockSpec((tm, D_MODEL), lambda i, j: (i, 0)),
        out_shape=jax.ShapeDtypeStruct((m, D_MODEL), F32),
        scratch_shapes=[pltpu.VMEM((tm, D_MODEL), BF16)],
        compiler_params=_cparams(("arbitrary", "arbitrary")),
        name="mlp",
    )(x, g, mod, mod, mod, wu, wd)


def _row_iota(shape):
    return lax.broadcasted_iota(jnp.int32, shape, 0)


def _lane_iota(shape):
    return lax.broadcasted_iota(jnp.int32, shape, 1)


def _cat(pieces, axis):
    return pieces[0] if len(pieces) == 1 else jnp.concatenate(pieces, axis=axis)


def _cumsum_rows(x, lseq):
    rows = x.shape[0]
    rr = _row_iota((rows, rows))
    cc = _lane_iota((rows, rows))
    keep = cc <= rr
    if lseq < rows:
        keep = jnp.logical_and(keep, (rr // lseq) == (cc // lseq))
    tri = jnp.where(keep, 1.0, 0.0).astype(BF16)
    hi = x.astype(BF16)
    rem = x - hi.astype(F32)
    mid = rem.astype(BF16)
    lo = (rem - mid.astype(F32)).astype(BF16)
    return (jnp.dot(tri, hi, preferred_element_type=F32) + jnp.dot(tri, mid, preferred_element_type=F32)
            + jnp.dot(tri, lo, preferred_element_type=F32))


def _seq_last_rows(x, nseq, lseq):
    return _cat([jnp.broadcast_to(x[(b + 1) * lseq - 1:(b + 1) * lseq, :], (lseq, x.shape[1]))
                 for b in range(nseq)], 0)


def _hgrn_body(pm_ref, par_ref, st_ref, *, nseq, lseq):
    rows = TILE_ROWS
    dh = HGRN_DH
    aq = pm_ref[:, COL_AQ:COL_AQ + W_GROUP]
    xf = pm_ref[:, COL_AF:COL_AF + W_GROUP]
    v = pm_ref[:, COL_AI:COL_AI + W_GROUP]
    q = aq * _sigmoid(aq)
    e = jnp.exp(-jnp.abs(xf))
    inv = 1.0 / (1.0 + e)
    log_sig = jnp.minimum(xf, 0.0) - jnp.log1p(e)
    log_lb = par_ref[0:1, :]
    bterm = par_ref[1:2, :] + log_sig
    lf = jnp.maximum(log_lb, bterm) + jnp.log1p(jnp.exp(-jnp.abs(log_lb - bterm)))
    one_m_lb = par_ref[2:3, :]
    k = one_m_lb * jnp.where(xf >= 0.0, e * inv, inv)
    f = par_ref[4:5, :] + one_m_lb * jnp.where(xf >= 0.0, inv, e * inv)
    cum = _cumsum_rows(lf, lseq)

    row = _row_iota((rows, W_GROUP))
    heads = [slice(hd * dh, (hd + 1) * dh) for hd in range(HGRN_HEADS)]
    o = [jnp.zeros((rows, dh), F32) for _ in heads]

    sub = min(SUBLANES, lseq)
    fm = jnp.where((row & (sub - 1)) == 0, 0.0, f)
    kg = k
    vs = v
    for d in range(sub):
        if d > 0:
            kg = fm * pltpu.roll(kg, 1, axis=0)
            vs = pltpu.roll(vs, 1, axis=0)
        term = q * kg
        for hd, sl in enumerate(heads):
            o[hd] = o[hd] + jnp.sum(term[:, sl], axis=-1, keepdims=True) * vs[:, sl]

    levels = []
    h = lseq // 2
    while h >= sub:
        levels.append(h)
        h //= 2
    if levels:
        rr = _row_iota((rows, rows))
        cc = _lane_iota((rows, rows))
        p = [jnp.zeros((rows, rows), F32) for _ in heads]
        for h in levels:
            upper = (row & h) != 0
            refm = _cat([jnp.broadcast_to(cum[jb * 2 * h + h - 1:jb * 2 * h + h, :], (2 * h, W_GROUP))
                         for jb in range(rows // (2 * h))], 0)
            x = jnp.exp(jnp.where(upper, cum - refm, refm - cum))
            a_side = jnp.where(upper, q * x, 0.0).astype(BF16)
            b_side = jnp.where(upper, 0.0, k * x).astype(BF16)
            if 2 * h < rows:
                same = (rr // (2 * h)) == (cc // (2 * h))
            for hd, sl in enumerate(heads):
                s = lax.dot_general(a_side[:, sl], b_side[:, sl], _NT, preferred_element_type=F32)
                if 2 * h < rows:
                    s = jnp.where(same, s, 0.0)
                p[hd] = p[hd] + s
        for hd, sl in enumerate(heads):
            o[hd] = o[hd] + _dot(p[hd], v[:, sl])

    qe = (q * jnp.exp(cum)).astype(BF16)
    lastm = _seq_last_rows(cum, nseq, lseq)
    kd = k * jnp.exp(lastm - cum)
    dec = jnp.exp(lastm)
    vb = v.astype(BF16)
    seq_of_row = row // lseq
    for hd, sl in enumerate(heads):
        inter = []
        for b in range(nseq):
            rs = slice(b * lseq, (b + 1) * lseq)
            st = st_ref[b, hd]
            inter.append(lax.dot_general(qe[rs, sl], st.astype(BF16), _NT, preferred_element_type=F32))
            kdb = kd if nseq == 1 else jnp.where(seq_of_row == b, kd, 0.0)
            upd = lax.dot_general(vb[:, sl], kdb[:, sl].astype(BF16), _TN, preferred_element_type=F32)
            st_ref[b, hd] = st * dec[b * lseq:b * lseq + 1, sl] + upd
        o[hd] = o[hd] + _cat(inter, 0)

    gate = _sigmoid(pm_ref[:, COL_AG:COL_AG + W_GROUP])
    out = []
    for hd, sl in enumerate(heads):
        ms = jnp.mean(o[hd] * o[hd], axis=-1, keepdims=True)
        out.append(o[hd] * lax.rsqrt(ms + EPS) * par_ref[3:4, sl] * gate[:, sl])
    return out


def _pool_body(pm_ref, pw_ref, sc_ref, hout_ref, ext, i, *, nseq, lseq, pos0):
    stride = lseq + HIST_PAD
    u = pm_ref[:, COL_PU:COL_PU + W_GROUP]
    for b in range(nseq):
        base = b * stride
        ext[base + HIST_PAD - POOL_BUF:base + HIST_PAD, :] = hout_ref[b]
        ext[base + HIST_PAD:base + HIST_PAD + lseq, :] = u[b * lseq:(b + 1) * lseq, :]

    local = _row_iota((TILE_ROWS, POOL_CH)) & (lseq - 1)
    posn = pos0 + i * lseq + local
    out = []
    for gi, win in enumerate(POOL_WINDOWS):
        cs = slice(gi * POOL_CH, (gi + 1) * POOL_CH)
        pieces = []
        for b in range(nseq):
            base = b * stride + HIST_PAD
            s = ext[base:base + lseq, cs]
            for j in range(1, win):
                s = s + ext[pl.ds(base - j, lseq), cs]
            pieces.append(s)
        cnt = jnp.minimum(posn + 1, win).astype(F32)
        pooled = _cat(pieces, 0) / cnt - u[:, cs]
        out.append(_dot(pooled, pw_ref[gi]) * sc_ref[:, cs])

    for b in range(nseq):
        base = b * stride
        hout_ref[b] = ext[base + lseq + HIST_PAD - POOL_BUF:base + lseq + HIST_PAD, :]
    return out


def _expand_heads(z, emat):
    hi = z.astype(BF16)
    lo = (z - hi.astype(F32)).astype(BF16)
    return (jnp.dot(hi, emat, preferred_element_type=F32) + jnp.dot(lo, emat, preferred_element_type=F32))


def _ssd_body(pm_ref, pt_ref, cw_ref, cb_ref, par_ref, par5_ref, cout_ref, st_ref, ext, *, nseq, lseq):
    rows = TILE_ROWS
    hist = SSM_CONV - 1
    stride = lseq + SUBLANES
    gw = SSM_N

    xbc = pm_ref[:, COL_XBC:COL_XBC + SSM_CONV_DIM]
    for b in range(nseq):
        base = b * stride
        ext[base + SUBLANES - hist:base + SUBLANES, :] = cout_ref[b]
        ext[base + SUBLANES:base + SUBLANES + lseq, :] = xbc[b * lseq:(b + 1) * lseq, :]
    acc = None
    for j in range(SSM_CONV):
        sh = _cat([ext[pl.ds(b * stride + SUBLANES - hist + j, lseq), :] for b in range(nseq)], 0)
        t = sh * cw_ref[j:j + 1, :]
        acc = t if acc is None else acc + t
    for b in range(nseq):
        base = b * stride
        cout_ref[b] = ext[base + lseq + SUBLANES - hist:base + lseq + SUBLANES, :]
    acc = acc + cb_ref[...]
    conv = acc * _sigmoid(acc)
    x = conv[:, 0:W_GROUP]
    bm = conv[:, W_GROUP:W_GROUP + SSM_GROUPS * gw]
    cm = conv[:, W_GROUP + SSM_GROUPS * gw:]

    pre = pt_ref[:, COL_DT:COL_DT + LANES] + par_ref[0:1, :]
    dt = jnp.maximum(pre, 0.0) + jnp.log1p(jnp.exp(-jnp.abs(pre)))
    a = dt * (-jnp.exp(par_ref[1:2, :]))
    cum = _cumsum_rows(a, lseq)
    cum_t = cum.T
    dt_t = dt.T

    rr = _row_iota((rows, rows))
    cc = _lane_iota((rows, rows))
    valid = cc <= rr
    if nseq > 1:
        valid = jnp.logical_and(valid, (rr // lseq) == (cc // lseq))

    erow = _row_iota((LANES, W_GROUP))
    ecol = _lane_iota((LANES, W_GROUP))
    emat = jnp.where(ecol // SSM_P == erow, 1.0, 0.0).astype(BF16)

    lane = _lane_iota((rows, LANES))
    lo_half = lane < SSM_P
    heads_per_group = SSM_HEADS // SSM_GROUPS
    y_chunks = []
    for g in range(SSM_GROUPS):
        gs = slice(g * gw, (g + 1) * gw)
        cb = lax.dot_general(cm[:, gs].astype(BF16), bm[:, gs].astype(BF16), _NT, preferred_element_type=F32)
        for jc in range(heads_per_group // 2):
            chunk = g * (heads_per_group // 2) + jc
            xc = x[:, chunk * LANES:(chunk + 1) * LANES]
            yc = None
            for half in range(2):
                r = 2 * chunk + half
                seg = jnp.broadcast_to(cum[:, r:r + 1], (rows, rows)) - cum_t[r:r + 1, :]
                wts = cb * jnp.exp(jnp.where(valid, seg, NEG_BIG)) * dt_t[r:r + 1, :]
                xm = jnp.where(lo_half, xc, 0.0) if half == 0 else jnp.where(lo_half, 0.0, xc)
                t = _dot(wts, xm)
                yc = t if yc is None else yc + t
            y_chunks.append(yc)
    y = jnp.concatenate(y_chunks, axis=1)

    lastm = _seq_last_rows(cum, nseq, lseq)
    e_cum = _expand_heads(jnp.exp(cum), emat)
    e_wst = _expand_heads(dt * jnp.exp(lastm - cum), emat)
    e_dec = _expand_heads(jnp.exp(lastm), emat)
    xs = x * e_wst
    row5 = _row_iota((rows, W_GROUP))
    inter_groups = []
    pw = heads_per_group * SSM_P
    for g in range(SSM_GROUPS):
        gs = slice(g * gw, (g + 1) * gw)
        ps = slice(g * pw, (g + 1) * pw)
        inter = []
        for b in range(nseq):
            rs = slice(b * lseq, (b + 1) * lseq)
            st = st_ref[b, g]
            inter.append(_dot(cm[rs, gs], st))
            xsb = xs if nseq == 1 else jnp.where(row5 // lseq == b, xs, 0.0)
            upd = _dot_tn(bm[:, gs], xsb[:, ps])
            st_ref[b, g] = st * e_dec[b * lseq:b * lseq + 1, ps] + upd
        inter_groups.append(_cat(inter, 0))
    y = y + jnp.concatenate(inter_groups, axis=1) * e_cum
    y = y + x * par5_ref[0:1, :]
    z = pm_ref[:, COL_CZ:COL_CZ + W_GROUP]
    y = y * (z * _sigmoid(z))
    ms = jnp.mean(y * y, axis=-1, keepdims=True)
    return y * lax.rsqrt(ms + EPS) * par5_ref[1:2, :]


def _head_rmsnorm(x, g, lo_half):
    sq = x * x
    s_lo = jnp.sum(jnp.where(lo_half, sq, 0.0), axis=-1, keepdims=True)
    s_hi = jnp.sum(jnp.where(lo_half, 0.0, sq), axis=-1, keepdims=True)
    ms = jnp.where(lo_half, s_lo, s_hi) * (1.0 / ATTN_DH)
    return x * lax.rsqrt(ms + EPS) * g


def _rope(x, cos, sin_signed, upper_half):
    partner = jnp.where(upper_half, pltpu.roll(x, ATTN_DH // 2, axis=1), pltpu.roll(x, LANES - ATTN_DH // 2, axis=1))
    return x * cos + partner * sin_signed


def _dup_head(x, g, lane):
    own = jnp.where((lane // ATTN_DH) == g, x, 0.0)
    return own + pltpu.roll(own, ATTN_DH, axis=1)


def _swa_body(pt_ref, cos_ref, sin_ref, par_ref, sink_ref, kout_ref, vout_ref, i, *, nseq, lseq, has_state):
    rows = TILE_ROWS
    rep = ATTN_HEADS // ATTN_KV
    lane = _lane_iota((rows, LANES))
    lo_half = lane < ATTN_DH
    upper_half = (lane & (ATTN_DH // 2)) != 0
    cos = cos_ref[...]
    sin = sin_ref[...]
    kn = _rope(_head_rmsnorm(pt_ref[:, COL_DK:COL_DK + LANES], par_ref[1:2, :], lo_half), cos, sin, upper_half)
    vn = pt_ref[:, COL_DV:COL_DV + LANES]
    scale = ATTN_DH ** -0.5
    qs = []
    for c in range(ATTN_HEADS // 2):
        qc = pt_ref[:, COL_DQ + c * LANES:COL_DQ + (c + 1) * LANES]
        qs.append(_rope(_head_rmsnorm(qc, par_ref[0:1, :], lo_half), cos, sin, upper_half) * scale)

    mq = rep * lseq
    qrow = _row_iota((mq, LANES))
    q_local = qrow & (lseq - 1)
    kcol = _lane_iota((mq, LANES))
    hist_ok = jnp.logical_or(has_state, i > 0)
    hist_valid = jnp.logical_and(kcol > q_local, hist_ok)
    lane_w = _lane_iota((WINDOW, LANES))

    out_rows = [[None] * nseq for _ in range(ATTN_HEADS // 2)]
    for g in range(ATTN_KV):
        k_new = _dup_head(kn, g, lane).astype(BF16)
        v_new = _dup_head(vn, g, lane).astype(BF16)
        for b in range(nseq):
            rs = slice(b * lseq, (b + 1) * lseq)
            q4 = []
            sink_rows = []
            for r in range(rep):
                hidx = g * rep + r
                qc = qs[hidx // 2][rs, :]
                lo_l = lo_half[0:lseq, :]
                q4.append(jnp.where(lo_l, qc, 0.0) if hidx % 2 == 0 else jnp.where(lo_l, 0.0, qc))
                sink_rows.append(jnp.broadcast_to(sink_ref[hidx:hidx + 1, 0:1], (lseq, 1)))
            q4 = jnp.concatenate(q4, axis=0).astype(BF16)
            sink = jnp.concatenate(sink_rows, axis=0)
            k_hist = _dup_head(kout_ref[b], g, lane_w).astype(BF16)
            v_hist = _dup_head(vout_ref[b], g, lane_w).astype(BF16)
            s_h = lax.dot_general(q4, k_hist, _NT, preferred_element_type=F32)
            s_n = lax.dot_general(q4, k_new, _NT, preferred_element_type=F32)
            s_h = jnp.where(hist_valid, s_h, NEG_BIG)
            new_valid = jnp.logical_and(kcol // lseq == b, (kcol & (lseq - 1)) <= q_local)
            s_n = jnp.where(new_valid, s_n, NEG_BIG)
            m = jnp.maximum(jnp.maximum(jnp.max(s_h, axis=-1, keepdims=True),
                                        jnp.max(s_n, axis=-1, keepdims=True)), sink)
            p_h = jnp.exp(s_h - m)
            p_n = jnp.exp(s_n - m)
            den = (jnp.sum(p_h, axis=-1, keepdims=True) + jnp.sum(p_n, axis=-1, keepdims=True)
                   + jnp.exp(sink - m))
            o4 = (jnp.dot(p_h.astype(BF16), v_hist, preferred_element_type=F32)
                  + jnp.dot(p_n.astype(BF16), v_new, preferred_element_type=F32)) / den
            for pair in range(rep // 2):
                c = (g * rep) // 2 + pair
                o_lo = o4[(2 * pair) * lseq:(2 * pair + 1) * lseq, :]
                o_hi = o4[(2 * pair + 1) * lseq:(2 * pair + 2) * lseq, :]
                out_rows[c][b] = jnp.where(lo_half[0:lseq, :], o_lo, o_hi)

    for b in range(nseq):
        rs = slice(b * lseq, (b + 1) * lseq)
        if lseq < WINDOW:
            keep_k = kout_ref[b, lseq:WINDOW, :]
            keep_v = vout_ref[b, lseq:WINDOW, :]
            kout_ref[b, 0:WINDOW - lseq, :] = keep_k
            vout_ref[b, 0:WINDOW - lseq, :] = keep_v
        kout_ref[b, WINDOW - lseq:WINDOW, :] = kn[rs, :]
        vout_ref[b, WINDOW - lseq:WINDOW, :] = vn[rs, :]
    return [_cat(out_rows[c], 0) for c in range(ATTN_HEADS // 2)]


N_STATES = 6


def _mix_kernel(*refs, nseq, lseq, has_state, pos0):
    (pm_ref, pt_ref, x_ref, gate_ref, w_ref, cos_ref, sin_ref, hpar_ref, pw_ref, psc_ref, cw_ref, cb_ref,
     spar_ref, spar5_ref, apar_ref, sink_ref) = refs[:16]
    rest = refs[16:]
    if has_state:
        init_refs, rest = rest[:N_STATES], rest[N_STATES:]
    else:
        init_refs = (None,) * N_STATES
    o_ref = rest[0]
    state_refs = rest[1:1 + N_STATES]
    pool_ext, conv_ext = rest[1 + N_STATES:]
    nh_ref, np_ref, nc_ref, ns_ref, nk_ref, nv_ref = state_refs
    i = pl.program_id(1)

    @pl.when(i == 0)
    def _():
        for dst, src in zip(state_refs, init_refs):
            dst[...] = jnp.zeros_like(dst) if src is None else src[...]

    ya = _hgrn_body(pm_ref, hpar_ref, nh_ref, nseq=nseq, lseq=lseq)
    yb = _pool_body(pm_ref, pw_ref, psc_ref, np_ref, pool_ext, i, nseq=nseq, lseq=lseq, pos0=pos0)
    yc = _ssd_body(pm_ref, pt_ref, cw_ref, cb_ref, spar_ref, spar5_ref, nc_ref, ns_ref, conv_ext, nseq=nseq, lseq=lseq)
    yd = _swa_body(pt_ref, cos_ref, sin_ref, apar_ref, sink_ref, nk_ref, nv_ref, i, nseq=nseq, lseq=lseq,
                   has_state=has_state)
    y = jnp.concatenate([t.astype(BF16) for t in ya + yb + [yc] + yd], axis=1)
    mixed = jnp.dot(y, w_ref[...], preferred_element_type=F32)
    o_ref[...] = x_ref[...] + gate_ref[...] * mixed


def _mix_call(grp, l, x, proj, tail, mod, w_out, cos, sin, pars, states):
    nseq, lseq, no, nt = grp["nseq"], grp["lseq"], grp["no"], grp["nt"]
    has_state = states is not None
    hpar, pool_w, pool_sc, conv_w, conv_b, spar, spar5, apar, sink = pars
    tails = [(HGRN_HEADS, HGRN_DH, HGRN_DH), (POOL_BUF, W_GROUP), (SSM_CONV - 1, SSM_CONV_DIM),
             (SSM_GROUPS, SSM_N, (SSM_HEADS // SSM_GROUPS) * SSM_P),
             (WINDOW, ATTN_KV * ATTN_DH), (WINDOW, ATTN_KV * ATTN_DH)]

    def row_spec(width):
        return pl.BlockSpec((TILE_ROWS, width), lambda o, i: (o * nt + i, 0))

    in_specs = [
        row_spec(MAIN_WIDTH), row_spec(TAIL_WIDTH), row_spec(D_MODEL),
        grp["mod_spec"](l, 2, TILE_ROWS, lambda o, i: o * nt + i),
        _layer_spec(l, (D_MODEL, D_MODEL), pipeline_mode=pl.Buffered(1)),
        pl.BlockSpec((TILE_ROWS, LANES), lambda o, i: (i, 0)),
        pl.BlockSpec((TILE_ROWS, LANES), lambda o, i: (i, 0)),
        _layer_spec(l, (8, W_GROUP)), _layer_spec(l, (len(POOL_WINDOWS), POOL_CH, POOL_CH)),
        _layer_spec(l, (1, W_GROUP)), _layer_spec(l, (SSM_CONV, SSM_CONV_DIM)), _layer_spec(l, (1, SSM_CONV_DIM)),
        _layer_spec(l, (8, LANES)), _layer_spec(l, (8, W_GROUP)), _layer_spec(l, (8, LANES)), _layer_spec(l, (8, LANES)),
    ]
    args = [proj, tail, x, mod, w_out, cos, sin, hpar, pool_w, pool_sc, conv_w, conv_b, spar, spar5, apar, sink]
    if has_state:
        for t in tails:
            zeros = (0,) * len(t)
            in_specs.append(pl.BlockSpec((None, nseq) + t, lambda o, i, zeros=zeros: (l, o) + zeros,
                                         pipeline_mode=pl.Buffered(1)))
        args += list(states)
    out_specs = [row_spec(D_MODEL)]
    out_shape = [jax.ShapeDtypeStruct((grp["m"], D_MODEL), F32)]
    state_mode = dict(pipeline_mode=pl.Buffered(1)) if nseq > 1 else {}
    for t in tails:
        zeros = (0,) * len(t)
        out_specs.append(pl.BlockSpec((nseq,) + t, lambda o, i, zeros=zeros: (o,) + zeros, **state_mode))
        out_shape.append(jax.ShapeDtypeStruct((grp["nb"],) + t, F32))
    return pl.pallas_call(
        functools.partial(_mix_kernel, nseq=nseq, lseq=lseq, has_state=has_state, pos0=grp["pos0"]),
        grid=(no, nt),
        in_specs=in_specs,
        out_specs=out_specs,
        out_shape=out_shape,
        scratch_shapes=[pltpu.VMEM((nseq * (lseq + HIST_PAD), W_GROUP), F32),
                        pltpu.VMEM((nseq * (lseq + SUBLANES), SSM_CONV_DIM), F32)],
        compiler_params=_cparams(("arbitrary", "arbitrary")),
        name="mix",
    )(*args)


def _front_kernel(*refs, nsub, pos0):
    (x_ref, g_ref, mod_ref, wm_ref, wt_ref, w_ref, cos_ref, sin_ref, hpar_ref, pw_ref, psc_ref, cw_ref, cb_ref,
     spar_ref, spar5_ref, apar_ref, sink_ref) = refs[:17]
    o_ref = refs[17]
    state_refs = refs[18:18 + N_STATES]
    pm_scr, pt_scr = refs[18 + N_STATES:20 + N_STATES]
    ext_refs = refs[20 + N_STATES:]
    rows = TILE_ROWS
    i = pl.program_id(0)

    @pl.when(i == 0)
    def _():
        for dst in state_refs:
            dst[...] = jnp.zeros_like(dst)

    h = _cat([_modulated_norm(x_ref[s], g_ref[...], mod_ref[s * N_MOD + 1], mod_ref[s * N_MOD + 0]).astype(BF16)
              for s in range(nsub)], 0)
    pm_scr[...] = jnp.dot(h, wm_ref[...], preferred_element_type=F32)
    pt_scr[...] = jnp.dot(h, wt_ref[...], preferred_element_type=F32)

    ys = []
    for s in range(nsub):
        pm = pm_scr.at[pl.ds(s * rows, rows)]
        pt = pt_scr.at[pl.ds(s * rows, rows)]
        nh, npool, nc, ns, nk, nv = (r.at[pl.ds(s, 1)] for r in state_refs)
        ya = _hgrn_body(pm, hpar_ref, nh, nseq=1, lseq=rows)
        yb = _pool_body(pm, pw_ref, psc_ref, npool, ext_refs[2 * s], i, nseq=1, lseq=rows, pos0=pos0)
        yc = _ssd_body(pm, pt, cw_ref, cb_ref, spar_ref, spar5_ref, nc, ns, ext_refs[2 * s + 1], nseq=1, lseq=rows)
        yd = _swa_body(pt, cos_ref, sin_ref, apar_ref, sink_ref, nk, nv, i, nseq=1, lseq=rows, has_state=False)
        ys.append(jnp.concatenate([t.astype(BF16) for t in ya + yb + [yc] + yd], axis=1))
    mixed = jnp.dot(_cat(ys, 0), w_ref[...], preferred_element_type=F32)
    for s in range(nsub):
        o_ref[s] = x_ref[s] + mod_ref[s * N_MOD + 2] * mixed[s * rows:(s + 1) * rows, :]


def _front_call(grp, l, x, g, mod, w_in_b, w_tail, w_out, cos, sin, pars):
    nb, nt = grp["nb"], grp["nt"]
    hpar, pool_w, pool_sc, conv_w, conv_b, spar, spar5, apar, sink = pars
    tails = [(HGRN_HEADS, HGRN_DH, HGRN_DH), (POOL_BUF, W_GROUP), (SSM_CONV - 1, SSM_CONV_DIM),
             (SSM_GROUPS, SSM_N, (SSM_HEADS // SSM_GROUPS) * SSM_P),
             (WINDOW, ATTN_KV * ATTN_DH), (WINDOW, ATTN_KV * ATTN_DH)]
    once = dict(pipeline_mode=pl.Buffered(1))
    x_spec = pl.BlockSpec((nb, TILE_ROWS, D_MODEL), lambda i: (0, i, 0))
    in_specs = [
        x_spec, _layer_spec(l, (1, D_MODEL)), _layer_spec(l, (nb * N_MOD, 1, D_MODEL)),
        _layer_spec(l, (D_MODEL, MAIN_WIDTH), **once), _layer_spec(l, (D_MODEL, TAIL_WIDTH), **once),
        _layer_spec(l, (D_MODEL, D_MODEL), **once),
        pl.BlockSpec((TILE_ROWS, LANES), lambda i: (i, 0)), pl.BlockSpec((TILE_ROWS, LANES), lambda i: (i, 0)),
        _layer_spec(l, (8, W_GROUP)), _layer_spec(l, (len(POOL_WINDOWS), POOL_CH, POOL_CH)),
        _layer_spec(l, (1, W_GROUP)), _layer_spec(l, (SSM_CONV, SSM_CONV_DIM)), _layer_spec(l, (1, SSM_CONV_DIM)),
        _layer_spec(l, (8, LANES)), _layer_spec(l, (8, W_GROUP)), _layer_spec(l, (8, LANES)), _layer_spec(l, (8, LANES)),
    ]
    args = [x, g, mod, w_in_b, w_tail, w_out, cos, sin, hpar, pool_w, pool_sc, conv_w, conv_b, spar, spar5, apar, sink]
    out_specs = [x_spec]
    out_shape = [jax.ShapeDtypeStruct(x.shape, F32)]
    for t in tails:
        zeros = (0,) * (len(t) + 1)
        out_specs.append(pl.BlockSpec((nb,) + t, lambda i, zeros=zeros: zeros))
        out_shape.append(jax.ShapeDtypeStruct((nb,) + t, F32))
    scratch = [pltpu.VMEM((nb * TILE_ROWS, MAIN_WIDTH), F32), pltpu.VMEM((nb * TILE_ROWS, TAIL_WIDTH), F32)]
    for _ in range(nb):
        scratch += [pltpu.VMEM((TILE_ROWS + HIST_PAD, W_GROUP), F32),
                    pltpu.VMEM((TILE_ROWS + SUBLANES, SSM_CONV_DIM), F32)]
    return pl.pallas_call(
        functools.partial(_front_kernel, nsub=nb, pos0=grp["pos0"]),
        grid=(nt,),
        in_specs=in_specs,
        out_specs=out_specs,
        out_shape=out_shape,
        scratch_shapes=scratch,
        compiler_params=_cparams(("arbitrary",)),
        name="front",
    )(*args)


def _rope_tables(pos):
    half = ATTN_DH // 2
    inv = ROPE_THETA ** (-jnp.arange(half, dtype=F32) / half)
    ang = pos.astype(F32)[:, None] * inv[None]
    cos = jnp.tile(jnp.cos(ang), (1, LANES // half))
    sin = jnp.sin(ang)
    sin_signed = jnp.tile(jnp.concatenate([-sin, sin], axis=1), (1, LANES // ATTN_DH))
    return cos, sin_signed


def _pad_lanes(v, width):
    return jnp.pad(v, (0, width - v.shape[0]))


def _rows8(rows_list, width):
    out = jnp.zeros((8, width), F32)
    for r, v in enumerate(rows_list):
        out = out.at[r].set(v)
    return out


def _make_group(nb, seq_len, pos0, mod_rows, tm):
    m = nb * seq_len
    if seq_len >= TILE_ROWS:
        nseq, lseq = 1, TILE_ROWS
        no, nt = nb, seq_len // TILE_ROWS
    else:
        nseq, lseq = TILE_ROWS // seq_len, seq_len
        no, nt = m // TILE_ROWS, 1
    per_row = mod_rows.shape[-2] != 1

    def mod_spec(l, kind, tile, row_tile=lambda i, *_: i):
        if per_row:
            def imap(*idx):
                return (l, kind, row_tile(*idx), 0)
            return pl.BlockSpec((None, None, tile, D_MODEL), imap)

        def imap(*idx):
            return (l, (row_tile(*idx) * tile) // seq_len * N_MOD + kind, 0, 0)
        return pl.BlockSpec((None, None, 1, D_MODEL), imap)

    return dict(nb=nb, seq=seq_len, m=m, tm=tm, nseq=nseq, lseq=lseq, no=no, nt=nt, pos0=pos0, mod_spec=mod_spec)


def _trunk(grp, x, mod, states, cos, sin, wts):
    w_in_b, w_tail, w_out_b, w_up, w_down, norm1_g, norm2_g, mix_pars = wts
    outs = [[] for _ in range(N_STATES)]
    for l in range(DEPTH):
        if grp["nseq"] == 1 and states is None:
            x3 = x.reshape(grp["nb"], grp["seq"], D_MODEL)
            x3, *new_states = _front_call(grp, l, x3, norm1_g, mod, w_in_b, w_tail, w_out_b, cos, sin, mix_pars)
            x = x3.reshape(grp["m"], D_MODEL)
        else:
            proj, tail = _inproj_call(grp, l, x, norm1_g, mod, w_in_b, w_tail)
            x, *new_states = _mix_call(grp, l, x, proj, tail, mod, w_out_b, cos, sin, mix_pars, states)
        x = _mlp_call(grp, l, x, norm2_g, mod, w_up, w_down)
        for lst, val in zip(outs, new_states):
            lst.append(val)
    return x, [jnp.stack(o) for o in outs]


def _ssm_state_to_kernel(s):
    lead = s.shape[:-3]
    r = SSM_HEADS // SSM_GROUPS
    s = s.reshape(lead + (SSM_GROUPS, r, SSM_P, SSM_N))
    s = jnp.moveaxis(s, -1, -3)
    return s.reshape(lead + (SSM_GROUPS, SSM_N, r * SSM_P))


def _ssm_state_from_kernel(s):
    lead = s.shape[:-3]
    r = SSM_HEADS // SSM_GROUPS
    s = s.reshape(lead + (SSM_GROUPS, SSM_N, r, SSM_P))
    s = jnp.moveaxis(s, -3, -1)
    return s.reshape(lead + (SSM_HEADS, SSM_P, SSM_N))


def kernel(x_prompt, x_sample, c_prompt, c_sample, state_hgrn, state_pool, state_ssm, state_conv, cache_k, cache_v, norm1_g, norm2_g, w_ada, b_ada, w_in, hgrn_lb_logits, hgrn_norm_g, pool_w, pool_scale, conv_w, conv_b, dt_bias, a_log, d_skip, ssm_norm_g, q_norm_g, k_norm_g, sinks, w_out, w_up, w_down):
    bp, seq, _ = x_prompt.shape
    bs, dseq, _ = x_sample.shape

    w_in_b = w_in.astype(BF16)
    dt_end = MAIN_WIDTH + SSM_HEADS
    w_tail = jnp.concatenate(
        [w_in_b[:, :, dt_end:], w_in_b[:, :, MAIN_WIDTH:dt_end],
         jnp.zeros((DEPTH, D_MODEL, LANES - SSM_HEADS), BF16)], axis=-1)
    w_out_b = w_out.astype(BF16)
    pool_w_b = pool_w.astype(BF16)

    p = jax.nn.softmax(hgrn_lb_logits.astype(F32), axis=0)
    cs = jnp.cumsum(p, axis=0)
    lbs = cs - cs[:1]
    hgrn_par = jnp.stack([_rows8([jnp.log(lbs[l]), jnp.log1p(-lbs[l]), 1.0 - lbs[l], hgrn_norm_g[l], lbs[l]], W_GROUP)
                          for l in range(DEPTH)])
    ssd_par = jnp.stack([_rows8([_pad_lanes(dt_bias[l], LANES), _pad_lanes(a_log[l], LANES)], LANES)
                         for l in range(DEPTH)])
    ssd_par5 = jnp.stack([_rows8([jnp.repeat(d_skip[l], SSM_P), ssm_norm_g[l]], W_GROUP) for l in range(DEPTH)])
    swa_par = jnp.stack([_rows8([jnp.tile(q_norm_g[l], 2), jnp.tile(k_norm_g[l], 2)], LANES) for l in range(DEPTH)])
    sink_par = jnp.broadcast_to(sinks[:, :, None], (DEPTH, ATTN_HEADS, LANES))
    mix_pars = (hgrn_par, pool_w_b, pool_scale.reshape(DEPTH, 1, W_GROUP), conv_w,
                conv_b.reshape(DEPTH, 1, SSM_CONV_DIM), ssd_par, ssd_par5, swa_par, sink_par)
    wts = (w_in_b, w_tail, w_out_b, w_up, w_down, norm1_g.reshape(DEPTH, 1, D_MODEL),
           norm2_g.reshape(DEPTH, 1, D_MODEL), mix_pars)

    c_all = jnp.concatenate([c_prompt, c_sample], axis=0)
    mod_all = _ada_call(c_all, w_ada, b_ada)
    mod_p = mod_all[:, :bp].reshape(DEPTH, bp * N_MOD, 1, D_MODEL)
    mod_s = mod_all[:, bp:].reshape(DEPTH, bs, N_MOD, D_MODEL)
    mod_s = jnp.repeat(jnp.moveaxis(mod_s, 2, 1), dseq, axis=2)

    grp_p = _make_group(bp, seq, 0, mod_p, 1024)
    grp_s = _make_group(bs, dseq, PAST_LEN, mod_s, bs * dseq)

    cos_p, sin_p = _rope_tables(jnp.arange(seq))
    cos_s, sin_s = _rope_tables(PAST_LEN + (jnp.arange(TILE_ROWS) % dseq))

    y_p, st_p = _trunk(grp_p, x_prompt.reshape(bp * seq, D_MODEL), mod_p, None, cos_p, sin_p, wts)

    kv_flat = (DEPTH, bs, WINDOW, ATTN_KV * ATTN_DH)
    states = (jnp.swapaxes(state_hgrn, -1, -2), state_pool, state_conv, _ssm_state_to_kernel(state_ssm),
              cache_k.reshape(kv_flat), cache_v.reshape(kv_flat))
    y_s, st_s = _trunk(grp_s, x_sample.reshape(bs * dseq, D_MODEL), mod_s, states, cos_s, sin_s, wts)

    def finish(st, nb):
        n_h, n_p, n_c, n_s, n_k, n_v = st
        return (jnp.swapaxes(n_h, -1, -2), n_p, _ssm_state_from_kernel(n_s), n_c,
                n_k.reshape(DEPTH, nb, WINDOW, ATTN_KV, ATTN_DH), n_v.reshape(DEPTH, nb, WINDOW, ATTN_KV, ATTN_DH))

    return ((y_p.reshape(bp, seq, D_MODEL), y_s.reshape(bs, dseq, D_MODEL)) + finish(st_p, bp) + finish(st_s, bs))
```

```python
import functools

import jax
import jax.numpy as jnp
from jax import lax
from jax.experimental import pallas as pl
from jax.experimental.pallas import tpu as pltpu

F32 = jnp.float32
BF16 = jnp.bfloat16

D_MODEL = 2048
DEPTH = 4
PAST_LEN = 16384
W_GROUP = 512
HGRN_HEADS = 4
HGRN_DH = 128
POOL_WINDOWS = (2, 4, 8, 16)
POOL_CH = 128
POOL_BUF = 15
SSM_HEADS = 8
SSM_P = 64
SSM_N = 128
SSM_GROUPS = 2
SSM_CONV = 4
SSM_CONV_DIM = 1024
ATTN_HEADS = 8
ATTN_KV = 2
ATTN_DH = 64
WINDOW = 128
ROPE_THETA = 10000.0
D_FF = 4 * D_MODEL
N_MOD = 6
EPS = 1e-6

LANES = 128
SUBLANES = 8
TILE_ROWS = 128
HIST_PAD = 16
VMEM_LIMIT = 56 * 1024 * 1024

COL_AQ, COL_AF, COL_AI, COL_AG = 0, 512, 1024, 1536
COL_PU, COL_CZ, COL_XBC = 2048, 2560, 3072
MAIN_WIDTH = 4096
COL_DQ, COL_DK, COL_DV, COL_DT = 0, 512, 640, 768
TAIL_WIDTH = 896
NEG_BIG = -1e30

_NT = (((1,), (1,)), ((), ()))
_TN = (((0,), (0,)), ((), ()))


def _dot(a, b):
    return jnp.dot(a.astype(BF16), b.astype(BF16), preferred_element_type=F32)


def _dot_nt(a, b):
    return lax.dot_general(a.astype(BF16), b.astype(BF16), _NT, preferred_element_type=F32)


def _dot_tn(a, b):
    return lax.dot_general(a.astype(BF16), b.astype(BF16), _TN, preferred_element_type=F32)


def _sigmoid(x):
    return 1.0 / (1.0 + jnp.exp(-x))


def _cparams(sem):
    return pltpu.CompilerParams(dimension_semantics=sem, vmem_limit_bytes=VMEM_LIMIT)


def _ada_kernel(c_ref, w_ref, b_ref, o_ref):
    c = c_ref[...]
    s = c * _sigmoid(c)
    o_ref[0] = _dot(s, w_ref[0]) + b_ref[0]


def _ada_call(c_all, w_ada, b_ada):
    rows = c_all.shape[0]
    n = w_ada.shape[-1]
    tn = 1024
    return pl.pallas_call(
        _ada_kernel,
        grid=(DEPTH, n // tn),
        in_specs=[
            pl.BlockSpec((rows, D_MODEL), lambda l, j: (0, 0)),
            pl.BlockSpec((1, D_MODEL, tn), lambda l, j: (l, 0, j)),
            pl.BlockSpec((1, 1, tn), lambda l, j: (l, 0, j)),
        ],
        out_specs=pl.BlockSpec((1, rows, tn), lambda l, j: (l, 0, j)),
        out_shape=jax.ShapeDtypeStruct((DEPTH, rows, n), F32),
        compiler_params=_cparams(("arbitrary", "arbitrary")),
        name="ada_mod",
    )(c_all, w_ada, b_ada.reshape(DEPTH, 1, n))


def _modulated_norm(x, g, scale, shift):
    ms = jnp.mean(x * x, axis=-1, keepdims=True)
    y = x * lax.rsqrt(ms + EPS) * g
    return y * (1.0 + scale) + shift


def _layer_spec(l, shape, **kw):
    zeros = (0,) * len(shape)
    return pl.BlockSpec((None,) + tuple(shape), lambda *_: (l,) + zeros, **kw)


def _inproj_kernel(x_ref, g_ref, sc_ref, sh_ref, wm_ref, wt_ref, om_ref, ot_ref, h_ref):
    j = pl.program_id(1)
    n_main = pl.num_programs(1) - 1

    @pl.when(j == 0)
    def _():
        h_ref[...] = _modulated_norm(x_ref[...], g_ref[...], sc_ref[...], sh_ref[...]).astype(BF16)

    @pl.when(j < n_main)
    def _():
        om_ref[...] = jnp.dot(h_ref[...], wm_ref[...], preferred_element_type=F32)

    @pl.when(j == n_main)
    def _():
        ot_ref[...] = jnp.dot(h_ref[...], wt_ref[...], preferred_element_type=F32)


def _inproj_call(grp, l, x, g, mod, w_main, w_tail):
    m, tm = grp["m"], grp["tm"]
    tn = 1024
    n_main = MAIN_WIDTH // tn
    return pl.pallas_call(
        _inproj_kernel,
        grid=(m // tm, n_main + 1),
        in_specs=[
            pl.BlockSpec((tm, D_MODEL), lambda i, j: (i, 0)),
            _layer_spec(l, (1, D_MODEL)),
            grp["mod_spec"](l, 1, tm),
            grp["mod_spec"](l, 0, tm),
            pl.BlockSpec((None, D_MODEL, tn), lambda i, j: (l, 0, jnp.minimum(j, n_main - 1))),
            _layer_spec(l, (D_MODEL, TAIL_WIDTH)),
        ],
        out_specs=[pl.BlockSpec((tm, tn), lambda i, j: (i, jnp.minimum(j, n_main - 1))),
                   pl.BlockSpec((tm, TAIL_WIDTH), lambda i, j: (i, 0))],
        out_shape=[jax.ShapeDtypeStruct((m, MAIN_WIDTH), F32), jax.ShapeDtypeStruct((m, TAIL_WIDTH), F32)],
        scratch_shapes=[pltpu.VMEM((tm, D_MODEL), BF16)],
        compiler_params=_cparams(("arbitrary", "arbitrary")),
        name="in_proj",
    )(x, g, mod, mod, w_main, w_tail)


def _mlp_kernel(*refs, cast_next):
    if cast_next:
        h_ref, x_ref, gate_ref, wu_ref, wd_ref, fu_ref, fd_ref, o_ref, nu_ref, nd_ref = refs
    else:
        h_ref, x_ref, gate_ref, wu_ref, wd_ref, o_ref = refs
    j = pl.program_id(1)

    @pl.when(j == 0)
    def _():
        o_ref[...] = jnp.zeros_like(o_ref)

    u = jnp.dot(h_ref[...], wu_ref[...], preferred_element_type=F32)
    a = jnp.square(jnp.maximum(u, 0.0))
    o_ref[...] += jnp.dot(a.astype(BF16), wd_ref[...], preferred_element_type=F32)
    if cast_next:
        nu_ref[...] = fu_ref[...].astype(BF16)
        nd_ref[...] = fd_ref[...].astype(BF16)

    @pl.when(j == pl.num_programs(1) - 1)
    def _():
        o_ref[...] = x_ref[...] + gate_ref[...] * o_ref[...]


def _mlp_call(grp, l, h2, x, mod, wu_b, wd_b, w_up=None, w_down=None):
    m, tm = grp["m"], grp["tm"]
    tf = 512
    gi, gj = m // tm, D_FF // tf
    cast_next = w_up is not None
    in_specs = [
        pl.BlockSpec((tm, D_MODEL), lambda i, j: (i, 0)),
        pl.BlockSpec((tm, D_MODEL), lambda i, j: (i, 0), pipeline_mode=pl.Buffered(1)),
        grp["mod_spec"](l, 5, tm),
        pl.BlockSpec((D_MODEL, tf), lambda i, j: (0, j)),
        pl.BlockSpec((tf, D_MODEL), lambda i, j: (j, 0)),
    ]
    args = [h2, x, mod, wu_b, wd_b]
    out_specs = [pl.BlockSpec((tm, D_MODEL), lambda i, j: (i, 0))]
    out_shape = [jax.ShapeDtypeStruct((m, D_MODEL), F32)]
    if cast_next:
        up_rows, down_rows = D_MODEL // (gi * gj), D_FF // (gi * gj)
        in_specs += [pl.BlockSpec((None, up_rows, D_FF), lambda i, j: (l + 1, i * gj + j, 0)),
                     pl.BlockSpec((None, down_rows, D_MODEL), lambda i, j: (l + 1, i * gj + j, 0))]
        args += [w_up, w_down]
        out_specs += [pl.BlockSpec((up_rows, D_FF), lambda i, j: (i * gj + j, 0)),
                      pl.BlockSpec((down_rows, D_MODEL), lambda i, j: (i * gj + j, 0))]
        out_shape += [jax.ShapeDtypeStruct((D_MODEL, D_FF), BF16), jax.ShapeDtypeStruct((D_FF, D_MODEL), BF16)]
    return pl.pallas_call(
        functools.partial(_mlp_kernel, cast_next=cast_next),
        grid=(gi, gj),
        in_specs=in_specs,
        out_specs=out_specs,
        out_shape=out_shape,
        compiler_params=_cparams(("arbitrary", "arbitrary")),
        name="mlp",
    )(*args)


def _row_iota(shape):
    return lax.broadcasted_iota(jnp.int32, shape, 0)


def _lane_iota(shape):
    return lax.broadcasted_iota(jnp.int32, shape, 1)


def _cat(pieces, axis):
    return pieces[0] if len(pieces) == 1 else jnp.concatenate(pieces, axis=axis)


def _cumsum_rows(x, lseq):
    rows = x.shape[0]
    rr = _row_iota((rows, rows))
    cc = _lane_iota((rows, rows))
    keep = cc <= rr
    if lseq < rows:
        keep = jnp.logical_and(keep, (rr // lseq) == (cc // lseq))
    tri = jnp.where(keep, 1.0, 0.0).astype(BF16)
    hi = x.astype(BF16)
    rem = x - hi.astype(F32)
    mid = rem.astype(BF16)
    lo = (rem - mid.astype(F32)).astype(BF16)
    return (jnp.dot(tri, hi, preferred_element_type=F32) + jnp.dot(tri, mid, preferred_element_type=F32)
            + jnp.dot(tri, lo, preferred_element_type=F32))


def _seq_last_rows(x, nseq, lseq):
    return _cat([jnp.broadcast_to(x[(b + 1) * lseq - 1:(b + 1) * lseq, :], (lseq, x.shape[1]))
                 for b in range(nseq)], 0)


def _hgrn_body(pm_ref, par_ref, st_ref, *, nseq, lseq):
    rows = TILE_ROWS
    dh = HGRN_DH
    aq = pm_ref[:, COL_AQ:COL_AQ + W_GROUP]
    xf = pm_ref[:, COL_AF:COL_AF + W_GROUP]
    v = pm_ref[:, COL_AI:COL_AI + W_GROUP]
    q = aq * _sigmoid(aq)
    e = jnp.exp(-jnp.abs(xf))
    inv = 1.0 / (1.0 + e)
    log_sig = jnp.minimum(xf, 0.0) - jnp.log1p(e)
    log_lb = par_ref[0:1, :]
    bterm = par_ref[1:2, :] + log_sig
    lf = jnp.maximum(log_lb, bterm) + jnp.log1p(jnp.exp(-jnp.abs(log_lb - bterm)))
    one_m_lb = par_ref[2:3, :]
    k = one_m_lb * jnp.where(xf >= 0.0, e * inv, inv)
    f = par_ref[4:5, :] + one_m_lb * jnp.where(xf >= 0.0, inv, e * inv)
    cum = _cumsum_rows(lf, lseq)

    row = _row_iota((rows, W_GROUP))
    heads = [slice(hd * dh, (hd + 1) * dh) for hd in range(HGRN_HEADS)]
    o = [jnp.zeros((rows, dh), F32) for _ in heads]

    sub = min(SUBLANES, lseq)
    fm = jnp.where((row & (sub - 1)) == 0, 0.0, f)
    kg = k
    vs = v
    for d in range(sub):
        if d > 0:
            kg = fm * pltpu.roll(kg, 1, axis=0)
            vs = pltpu.roll(vs, 1, axis=0)
        term = q * kg
        for hd, sl in enumerate(heads):
            o[hd] = o[hd] + jnp.sum(term[:, sl], axis=-1, keepdims=True) * vs[:, sl]

    levels = []
    h = lseq // 2
    while h >= sub:
        levels.append(h)
        h //= 2
    if levels:
        rr = _row_iota((rows, rows))
        cc = _lane_iota((rows, rows))
        p = [jnp.zeros((rows, rows), F32) for _ in heads]
        for h in levels:
            upper = (row & h) != 0
            refm = _cat([jnp.broadcast_to(cum[jb * 2 * h + h - 1:jb * 2 * h + h, :], (2 * h, W_GROUP))
                         for jb in range(rows // (2 * h))], 0)
            x = jnp.exp(jnp.where(upper, cum - refm, refm - cum))
            a_side = jnp.where(upper, q * x, 0.0).astype(BF16)
            b_side = jnp.where(upper, 0.0, k * x).astype(BF16)
            if 2 * h < rows:
                same = (rr // (2 * h)) == (cc // (2 * h))
            for hd, sl in enumerate(heads):
                s = lax.dot_general(a_side[:, sl], b_side[:, sl], _NT, preferred_element_type=F32)
                if 2 * h < rows:
                    s = jnp.where(same, s, 0.0)
                p[hd] = p[hd] + s
        for hd, sl in enumerate(heads):
            o[hd] = o[hd] + _dot(p[hd], v[:, sl])

    qe = (q * jnp.exp(cum)).astype(BF16)
    lastm = _seq_last_rows(cum, nseq, lseq)
    kd = k * jnp.exp(lastm - cum)
    dec = jnp.exp(lastm)
    vb = v.astype(BF16)
    seq_of_row = row // lseq
    for hd, sl in enumerate(heads):
        inter = []
        for b in range(nseq):
            rs = slice(b * lseq, (b + 1) * lseq)
            st = st_ref[b, hd]
            inter.append(lax.dot_general(qe[rs, sl], st.astype(BF16), _NT, preferred_element_type=F32))
            kdb = kd if nseq == 1 else jnp.where(seq_of_row == b, kd, 0.0)
            upd = lax.dot_general(vb[:, sl], kdb[:, sl].astype(BF16), _TN, preferred_element_type=F32)
            st_ref[b, hd] = st * dec[b * lseq:b * lseq + 1, sl] + upd
        o[hd] = o[hd] + _cat(inter, 0)

    gate = _sigmoid(pm_ref[:, COL_AG:COL_AG + W_GROUP])
    out = []
    for hd, sl in enumerate(heads):
        ms = jnp.mean(o[hd] * o[hd], axis=-1, keepdims=True)
        out.append(o[hd] * lax.rsqrt(ms + EPS) * par_ref[3:4, sl] * gate[:, sl])
    return out


def _pool_body(pm_ref, pw_ref, sc_ref, hout_ref, ext, i, *, nseq, lseq, pos0):
    stride = lseq + HIST_PAD
    u = pm_ref[:, COL_PU:COL_PU + W_GROUP]
    for b in range(nseq):
        base = b * stride
        ext[base + HIST_PAD - POOL_BUF:base + HIST_PAD, :] = hout_ref[b]
        ext[base + HIST_PAD:base + HIST_PAD + lseq, :] = u[b * lseq:(b + 1) * lseq, :]

    local = _row_iota((TILE_ROWS, POOL_CH)) & (lseq - 1)
    posn = pos0 + i * lseq + local
    out = []
    for gi, win in enumerate(POOL_WINDOWS):
        cs = slice(gi * POOL_CH, (gi + 1) * POOL_CH)
        pieces = []
        for b in range(nseq):
            base = b * stride + HIST_PAD
            s = ext[base:base + lseq, cs]
            for j in range(1, win):
                s = s + ext[pl.ds(base - j, lseq), cs]
            pieces.append(s)
        cnt = jnp.minimum(posn + 1, win).astype(F32)
        pooled = _cat(pieces, 0) / cnt - u[:, cs]
        out.append(_dot(pooled, pw_ref[gi]) * sc_ref[:, cs])

    for b in range(nseq):
        base = b * stride
        hout_ref[b] = ext[base + lseq + HIST_PAD - POOL_BUF:base + lseq + HIST_PAD, :]
    return out


def _expand_heads(z, emat):
    hi = z.astype(BF16)
    lo = (z - hi.astype(F32)).astype(BF16)
    return (jnp.dot(hi, emat, preferred_element_type=F32) + jnp.dot(lo, emat, preferred_element_type=F32))


def _ssd_body(pm_ref, pt_ref, cw_ref, cb_ref, par_ref, par5_ref, cout_ref, st_ref, ext, *, nseq, lseq):
    rows = TILE_ROWS
    hist = SSM_CONV - 1
    stride = lseq + SUBLANES
    gw = SSM_N

    xbc = pm_ref[:, COL_XBC:COL_XBC + SSM_CONV_DIM]
    for b in range(nseq):
        base = b * stride
        ext[base + SUBLANES - hist:base + SUBLANES, :] = cout_ref[b]
        ext[base + SUBLANES:base + SUBLANES + lseq, :] = xbc[b * lseq:(b + 1) * lseq, :]
    acc = None
    for j in range(SSM_CONV):
        sh = _cat([ext[pl.ds(b * stride + SUBLANES - hist + j, lseq), :] for b in range(nseq)], 0)
        t = sh * cw_ref[j:j + 1, :]
        acc = t if acc is None else acc + t
    for b in range(nseq):
        base = b * stride
        cout_ref[b] = ext[base + lseq + SUBLANES - hist:base + lseq + SUBLANES, :]
    acc = acc + cb_ref[...]
    conv = acc * _sigmoid(acc)
    x = conv[:, 0:W_GROUP]
    bm = conv[:, W_GROUP:W_GROUP + SSM_GROUPS * gw]
    cm = conv[:, W_GROUP + SSM_GROUPS * gw:]

    pre = pt_ref[:, COL_DT:COL_DT + LANES] + par_ref[0:1, :]
    dt = jnp.maximum(pre, 0.0) + jnp.log1p(jnp.exp(-jnp.abs(pre)))
    a = dt * (-jnp.exp(par_ref[1:2, :]))
    cum = _cumsum_rows(a, lseq)
    cum_t = cum.T
    dt_t = dt.T

    rr = _row_iota((rows, rows))
    cc = _lane_iota((rows, rows))
    valid = cc <= rr
    if nseq > 1:
        valid = jnp.logical_and(valid, (rr // lseq) == (cc // lseq))

    erow = _row_iota((LANES, W_GROUP))
    ecol = _lane_iota((LANES, W_GROUP))
    emat = jnp.where(ecol // SSM_P == erow, 1.0, 0.0).astype(BF16)

    lane = _lane_iota((rows, LANES))
    lo_half = lane < SSM_P
    heads_per_group = SSM_HEADS // SSM_GROUPS
    y_chunks = []
    for g in range(SSM_GROUPS):
        gs = slice(g * gw, (g + 1) * gw)
        cb = lax.dot_general(cm[:, gs].astype(BF16), bm[:, gs].astype(BF16), _NT, preferred_element_type=F32)
        for jc in range(heads_per_group // 2):
            chunk = g * (heads_per_group // 2) + jc
            xc = x[:, chunk * LANES:(chunk + 1) * LANES]
            yc = None
            for half in range(2):
                r = 2 * chunk + half
                seg = jnp.broadcast_to(cum[:, r:r + 1], (rows, rows)) - cum_t[r:r + 1, :]
                wts = cb * jnp.exp(jnp.where(valid, seg, NEG_BIG)) * dt_t[r:r + 1, :]
                xm = jnp.where(lo_half, xc, 0.0) if half == 0 else jnp.where(lo_half, 0.0, xc)
                t = _dot(wts, xm)
                yc = t if yc is None else yc + t
            y_chunks.append(yc)
    y = jnp.concatenate(y_chunks, axis=1)

    lastm = _seq_last_rows(cum, nseq, lseq)
    e_cum = _expand_heads(jnp.exp(cum), emat)
    e_wst = _expand_heads(dt * jnp.exp(lastm - cum), emat)
    e_dec = _expand_heads(jnp.exp(lastm), emat)
    xs = x * e_wst
    row5 = _row_iota((rows, W_GROUP))
    inter_groups = []
    pw = heads_per_group * SSM_P
    for g in range(SSM_GROUPS):
        gs = slice(g * gw, (g + 1) * gw)
        ps = slice(g * pw, (g + 1) * pw)
        inter = []
        for b in range(nseq):
            rs = slice(b * lseq, (b + 1) * lseq)
            st = st_ref[b, g]
            inter.append(_dot(cm[rs, gs], st))
            xsb = xs if nseq == 1 else jnp.where(row5 // lseq == b, xs, 0.0)
            upd = _dot_tn(bm[:, gs], xsb[:, ps])
            st_ref[b, g] = st * e_dec[b * lseq:b * lseq + 1, ps] + upd
        inter_groups.append(_cat(inter, 0))
    y = y + jnp.concatenate(inter_groups, axis=1) * e_cum
    y = y + x * par5_ref[0:1, :]
    z = pm_ref[:, COL_CZ:COL_CZ + W_GROUP]
    y = y * (z * _sigmoid(z))
    ms = jnp.mean(y * y, axis=-1, keepdims=True)
    return y * lax.rsqrt(ms + EPS) * par5_ref[1:2, :]


def _head_rmsnorm(x, g, lo_half):
    sq = x * x
    s_lo = jnp.sum(jnp.where(lo_half, sq, 0.0), axis=-1, keepdims=True)
    s_hi = jnp.sum(jnp.where(lo_half, 0.0, sq), axis=-1, keepdims=True)
    ms = jnp.where(lo_half, s_lo, s_hi) * (1.0 / ATTN_DH)
    return x * lax.rsqrt(ms + EPS) * g


def _rope(x, cos, sin_signed, upper_half):
    partner = jnp.where(upper_half, pltpu.roll(x, ATTN_DH // 2, axis=1), pltpu.roll(x, LANES - ATTN_DH // 2, axis=1))
    return x * cos + partner * sin_signed


def _dup_head(x, g, lane):
    own = jnp.where((lane // ATTN_DH) == g, x, 0.0)
    return own + pltpu.roll(own, ATTN_DH, axis=1)


def _swa_body(pt_ref, cos_ref, sin_ref, par_ref, sink_ref, kout_ref, vout_ref, i, *, nseq, lseq, has_state):
    rows = TILE_ROWS
    rep = ATTN_HEADS // ATTN_KV
    lane = _lane_iota((rows, LANES))
    lo_half = lane < ATTN_DH
    upper_half = (lane & (ATTN_DH // 2)) != 0
    cos = cos_ref[...]
    sin = sin_ref[...]
    kn = _rope(_head_rmsnorm(pt_ref[:, COL_DK:COL_DK + LANES], par_ref[1:2, :], lo_half), cos, sin, upper_half)
    vn = pt_ref[:, COL_DV:COL_DV + LANES]
    scale = ATTN_DH ** -0.5
    qs = []
    for c in range(ATTN_HEADS // 2):
        qc = pt_ref[:, COL_DQ + c * LANES:COL_DQ + (c + 1) * LANES]
        qs.append(_rope(_head_rmsnorm(qc, par_ref[0:1, :], lo_half), cos, sin, upper_half) * scale)

    mq = rep * lseq
    qrow = _row_iota((mq, LANES))
    q_local = qrow & (lseq - 1)
    kcol = _lane_iota((mq, LANES))
    hist_ok = jnp.logical_or(has_state, i > 0)
    hist_valid = jnp.logical_and(kcol > q_local, hist_ok)
    lane_w = _lane_iota((WINDOW, LANES))

    out_rows = [[None] * nseq for _ in range(ATTN_HEADS // 2)]
    for g in range(ATTN_KV):
        k_new = _dup_head(kn, g, lane).astype(BF16)
        v_new = _dup_head(vn, g, lane).astype(BF16)
        for b in range(nseq):
            rs = slice(b * lseq, (b + 1) * lseq)
            q4 = []
            sink_rows = []
            for r in range(rep):
                hidx = g * rep + r
                qc = qs[hidx // 2][rs, :]
                lo_l = lo_half[0:lseq, :]
                q4.append(jnp.where(lo_l, qc, 0.0) if hidx % 2 == 0 else jnp.where(lo_l, 0.0, qc))
                sink_rows.append(jnp.broadcast_to(sink_ref[hidx:hidx + 1, 0:1], (lseq, 1)))
            q4 = jnp.concatenate(q4, axis=0).astype(BF16)
            sink = jnp.concatenate(sink_rows, axis=0)
            k_hist = _dup_head(kout_ref[b], g, lane_w).astype(BF16)
            v_hist = _dup_head(vout_ref[b], g, lane_w).astype(BF16)
            s_h = lax.dot_general(q4, k_hist, _NT, preferred_element_type=F32)
            s_n = lax.dot_general(q4, k_new, _NT, preferred_element_type=F32)
            s_h = jnp.where(hist_valid, s_h, NEG_BIG)
            new_valid = jnp.logical_and(kcol // lseq == b, (kcol & (lseq - 1)) <= q_local)
            s_n = jnp.where(new_valid, s_n, NEG_BIG)
            m = jnp.maximum(jnp.maximum(jnp.max(s_h, axis=-1, keepdims=True),
                                        jnp.max(s_n, axis=-1, keepdims=True)), sink)
            p_h = jnp.exp(s_h - m)
            p_n = jnp.exp(s_n - m)
            den = (jnp.sum(p_h, axis=-1, keepdims=True) + jnp.sum(p_n, axis=-1, keepdims=True)
                   + jnp.exp(sink - m))
            o4 = (jnp.dot(p_h.astype(BF16), v_hist, preferred_element_type=F32)
                  + jnp.dot(p_n.astype(BF16), v_new, preferred_element_type=F32)) / den
            for pair in range(rep // 2):
                c = (g * rep) // 2 + pair
                o_lo = o4[(2 * pair) * lseq:(2 * pair + 1) * lseq, :]
                o_hi = o4[(2 * pair + 1) * lseq:(2 * pair + 2) * lseq, :]
                out_rows[c][b] = jnp.where(lo_half[0:lseq, :], o_lo, o_hi)

    for b in range(nseq):
        rs = slice(b * lseq, (b + 1) * lseq)
        if lseq < WINDOW:
            keep_k = kout_ref[b, lseq:WINDOW, :]
            keep_v = vout_ref[b, lseq:WINDOW, :]
            kout_ref[b, 0:WINDOW - lseq, :] = keep_k
            vout_ref[b, 0:WINDOW - lseq, :] = keep_v
        kout_ref[b, WINDOW - lseq:WINDOW, :] = kn[rs, :]
        vout_ref[b, WINDOW - lseq:WINDOW, :] = vn[rs, :]
    return [_cat(out_rows[c], 0) for c in range(ATTN_HEADS // 2)]


N_STATES = 6


def _transpose_heads(st_ref, n):
    for b in range(n):
        for hd in range(HGRN_HEADS):
            st_ref[b, hd] = st_ref[b, hd].T


def _mix_kernel(*refs, nseq, lseq, has_state, pos0):
    (pm_ref, pt_ref, x_ref, gate_ref, g2_ref, sc2_ref, sh2_ref, w_ref, cos_ref, sin_ref, hpar_ref, pw_ref, psc_ref,
     cw_ref, cb_ref, spar_ref, spar5_ref, apar_ref, sink_ref) = refs[:19]
    rest = refs[19:]
    if has_state:
        init_refs, rest = rest[:N_STATES], rest[N_STATES:]
    else:
        init_refs = (None,) * N_STATES
    o_ref, h2_ref = rest[:2]
    state_refs = rest[2:2 + N_STATES]
    pool_ext, conv_ext = rest[2 + N_STATES:]
    nh_ref, np_ref, nc_ref, ns_ref, nk_ref, nv_ref = state_refs
    i = pl.program_id(1)

    @pl.when(i == 0)
    def _():
        for dst, src in zip(state_refs, init_refs):
            if src is None:
                dst[...] = jnp.zeros_like(dst)
            elif dst is nh_ref:
                for b in range(nseq):
                    for hd in range(HGRN_HEADS):
                        dst[b, hd] = src[b, hd].T
            else:
                dst[...] = src[...]

    ya = _hgrn_body(pm_ref, hpar_ref, nh_ref, nseq=nseq, lseq=lseq)
    yb = _pool_body(pm_ref, pw_ref, psc_ref, np_ref, pool_ext, i, nseq=nseq, lseq=lseq, pos0=pos0)
    yc = _ssd_body(pm_ref, pt_ref, cw_ref, cb_ref, spar_ref, spar5_ref, nc_ref, ns_ref, conv_ext, nseq=nseq, lseq=lseq)
    yd = _swa_body(pt_ref, cos_ref, sin_ref, apar_ref, sink_ref, nk_ref, nv_ref, i, nseq=nseq, lseq=lseq,
                   has_state=has_state)
    y = jnp.concatenate([t.astype(BF16) for t in ya + yb + [yc] + yd], axis=1)
    mixed = jnp.dot(y, w_ref[...], preferred_element_type=F32)
    x1 = x_ref[...] + gate_ref[...] * mixed
    o_ref[...] = x1
    h2_ref[...] = _modulated_norm(x1, g2_ref[...], sc2_ref[...], sh2_ref[...]).astype(BF16)

    @pl.when(i == pl.num_programs(1) - 1)
    def _():
        _transpose_heads(nh_ref, nseq)


def _mix_call(grp, l, x, proj, tail, g2, mod, w_out, cos, sin, pars, states):
    nseq, lseq, no, nt = grp["nseq"], grp["lseq"], grp["no"], grp["nt"]
    has_state = states is not None
    hpar, pool_w, pool_sc, conv_w, conv_b, spar, spar5, apar, sink = pars
    tails = [(HGRN_HEADS, HGRN_DH, HGRN_DH), (POOL_BUF, W_GROUP), (SSM_CONV - 1, SSM_CONV_DIM),
             (SSM_GROUPS, SSM_N, (SSM_HEADS // SSM_GROUPS) * SSM_P),
             (WINDOW, ATTN_KV * ATTN_DH), (WINDOW, ATTN_KV * ATTN_DH)]

    def row_spec(width):
        return pl.BlockSpec((TILE_ROWS, width), lambda o, i: (o * nt + i, 0))

    in_specs = [
        row_spec(MAIN_WIDTH), row_spec(TAIL_WIDTH), row_spec(D_MODEL),
        grp["mod_spec"](l, 2, TILE_ROWS, lambda o, i: o * nt + i),
        _layer_spec(l, (1, D_MODEL)),
        grp["mod_spec"](l, 4, TILE_ROWS, lambda o, i: o * nt + i),
        grp["mod_spec"](l, 3, TILE_ROWS, lambda o, i: o * nt + i),
        _layer_spec(l, (D_MODEL, D_MODEL), pipeline_mode=pl.Buffered(1)),
        pl.BlockSpec((TILE_ROWS, LANES), lambda o, i: (i, 0)),
        pl.BlockSpec((TILE_ROWS, LANES), lambda o, i: (i, 0)),
        _layer_spec(l, (8, W_GROUP)), _layer_spec(l, (len(POOL_WINDOWS), POOL_CH, POOL_CH)),
        _layer_spec(l, (1, W_GROUP)), _layer_spec(l, (SSM_CONV, SSM_CONV_DIM)), _layer_spec(l, (1, SSM_CONV_DIM)),
        _layer_spec(l, (8, LANES)), _layer_spec(l, (8, W_GROUP)), _layer_spec(l, (8, LANES)), _layer_spec(l, (8, LANES)),
    ]
    args = [proj, tail, x, mod, g2, mod, mod, w_out, cos, sin, hpar, pool_w, pool_sc, conv_w, conv_b, spar, spar5,
            apar, sink]
    if has_state:
        for t in tails:
            zeros = (0,) * len(t)
            in_specs.append(pl.BlockSpec((None, nseq) + t, lambda o, i, zeros=zeros: (l, o) + zeros,
                                         pipeline_mode=pl.Buffered(1)))
        args += list(states)
    out_specs = [row_spec(D_MODEL), row_spec(D_MODEL)]
    out_shape = [jax.ShapeDtypeStruct((grp["m"], D_MODEL), F32), jax.ShapeDtypeStruct((grp["m"], D_MODEL), BF16)]
    state_mode = dict(pipeline_mode=pl.Buffered(1)) if nseq > 1 else {}
    for t in tails:
        zeros = (0,) * len(t)
        out_specs.append(pl.BlockSpec((nseq,) + t, lambda o, i, zeros=zeros: (o,) + zeros, **state_mode))
        out_shape.append(jax.ShapeDtypeStruct((grp["nb"],) + t, F32))
    return pl.pallas_call(
        functools.partial(_mix_kernel, nseq=nseq, lseq=lseq, has_state=has_state, pos0=grp["pos0"]),
        grid=(no, nt),
        in_specs=in_specs,
        out_specs=out_specs,
        out_shape=out_shape,
        scratch_shapes=[pltpu.VMEM((nseq * (lseq + HIST_PAD), W_GROUP), F32),
                        pltpu.VMEM((nseq * (lseq + SUBLANES), SSM_CONV_DIM), F32)],
        compiler_params=_cparams(("arbitrary", "arbitrary")),
        name="mix",
    )(*args)


def _front_kernel(*refs, nsub, pos0):
    (x_ref, g_ref, g2_ref, mod_ref, wm_ref, wt_ref, w_ref, cos_ref, sin_ref, hpar_ref, pw_ref, psc_ref, cw_ref,
     cb_ref, spar_ref, spar5_ref, apar_ref, sink_ref) = refs[:18]
    o_ref, h2_ref = refs[18:20]
    state_refs = refs[20:20 + N_STATES]
    pm_scr, pt_scr = refs[20 + N_STATES:22 + N_STATES]
    ext_refs = refs[22 + N_STATES:]
    rows = TILE_ROWS
    i = pl.program_id(0)

    @pl.when(i == 0)
    def _():
        for dst in state_refs:
            dst[...] = jnp.zeros_like(dst)

    h = _cat([_modulated_norm(x_ref[s], g_ref[...], mod_ref[s * N_MOD + 1], mod_ref[s * N_MOD + 0]).astype(BF16)
              for s in range(nsub)], 0)
    pm_scr[...] = jnp.dot(h, wm_ref[...], preferred_element_type=F32)
    pt_scr[...] = jnp.dot(h, wt_ref[...], preferred_element_type=F32)

    ys = []
    for s in range(nsub):
        pm = pm_scr.at[pl.ds(s * rows, rows)]
        pt = pt_scr.at[pl.ds(s * rows, rows)]
        nh, npool, nc, ns, nk, nv = (r.at[pl.ds(s, 1)] for r in state_refs)
        ya = _hgrn_body(pm, hpar_ref, nh, nseq=1, lseq=rows)
        yb = _pool_body(pm, pw_ref, psc_ref, npool, ext_refs[2 * s], i, nseq=1, lseq=rows, pos0=pos0)
        yc = _ssd_body(pm, pt, cw_ref, cb_ref, spar_ref, spar5_ref, nc, ns, ext_refs[2 * s + 1], nseq=1, lseq=rows)
        yd = _swa_body(pt, cos_ref, sin_ref, apar_ref, sink_ref, nk, nv, i, nseq=1, lseq=rows, has_state=False)
        ys.append(jnp.concatenate([t.astype(BF16) for t in ya + yb + [yc] + yd], axis=1))
    mixed = jnp.dot(_cat(ys, 0), w_ref[...], preferred_element_type=F32)
    for s in range(nsub):
        x1 = x_ref[s] + mod_ref[s * N_MOD + 2] * mixed[s * rows:(s + 1) * rows, :]
        o_ref[s] = x1
        h2_ref[s] = _modulated_norm(x1, g2_ref[...], mod_ref[s * N_MOD + 4], mod_ref[s * N_MOD + 3]).astype(BF16)

    @pl.when(i == pl.num_programs(0) - 1)
    def _():
        _transpose_heads(state_refs[0], nsub)


def _front_call(grp, l, x, g, g2, mod, w_in_b, w_tail, w_out, cos, sin, pars):
    nb, nt = grp["nb"], grp["nt"]
    hpar, pool_w, pool_sc, conv_w, conv_b, spar, spar5, apar, sink = pars
    tails = [(HGRN_HEADS, HGRN_DH, HGRN_DH), (POOL_BUF, W_GROUP), (SSM_CONV - 1, SSM_CONV_DIM),
             (SSM_GROUPS, SSM_N, (SSM_HEADS // SSM_GROUPS) * SSM_P),
             (WINDOW, ATTN_KV * ATTN_DH), (WINDOW, ATTN_KV * ATTN_DH)]
    once = dict(pipeline_mode=pl.Buffered(1))
    x_spec = pl.BlockSpec((nb, TILE_ROWS, D_MODEL), lambda i: (0, i, 0))
    in_specs = [
        x_spec, _layer_spec(l, (1, D_MODEL)), _layer_spec(l, (1, D_MODEL)), _layer_spec(l, (nb * N_MOD, 1, D_MODEL)),
        _layer_spec(l, (D_MODEL, MAIN_WIDTH), **once), _layer_spec(l, (D_MODEL, TAIL_WIDTH), **once),
        _layer_spec(l, (D_MODEL, D_MODEL), **once),
        pl.BlockSpec((TILE_ROWS, LANES), lambda i: (i, 0)), pl.BlockSpec((TILE_ROWS, LANES), lambda i: (i, 0)),
        _layer_spec(l, (8, W_GROUP)), _layer_spec(l, (len(POOL_WINDOWS), POOL_CH, POOL_CH)),
        _layer_spec(l, (1, W_GROUP)), _layer_spec(l, (SSM_CONV, SSM_CONV_DIM)), _layer_spec(l, (1, SSM_CONV_DIM)),
        _layer_spec(l, (8, LANES)), _layer_spec(l, (8, W_GROUP)), _layer_spec(l, (8, LANES)), _layer_spec(l, (8, LANES)),
    ]
    args = [x, g, g2, mod, w_in_b, w_tail, w_out, cos, sin, hpar, pool_w, pool_sc, conv_w, conv_b, spar, spar5, apar, sink]
    out_specs = [x_spec, x_spec]
    out_shape = [jax.ShapeDtypeStruct(x.shape, F32), jax.ShapeDtypeStruct(x.shape, BF16)]
    for t in tails:
        zeros = (0,) * (len(t) + 1)
        out_specs.append(pl.BlockSpec((nb,) + t, lambda i, zeros=zeros: zeros))
        out_shape.append(jax.ShapeDtypeStruct((nb,) + t, F32))
    scratch = [pltpu.VMEM((nb * TILE_ROWS, MAIN_WIDTH), F32), pltpu.VMEM((nb * TILE_ROWS, TAIL_WIDTH), F32)]
    for _ in range(nb):
        scratch += [pltpu.VMEM((TILE_ROWS + HIST_PAD, W_GROUP), F32),
                    pltpu.VMEM((TILE_ROWS + SUBLANES, SSM_CONV_DIM), F32)]
    return pl.pallas_call(
        functools.partial(_front_kernel, nsub=nb, pos0=grp["pos0"]),
        grid=(nt,),
        in_specs=in_specs,
        out_specs=out_specs,
        out_shape=out_shape,
        scratch_shapes=scratch,
        compiler_params=_cparams(("arbitrary",)),
        name="front",
    )(*args)


def _rope_tables(pos):
    half = ATTN_DH // 2
    inv = ROPE_THETA ** (-jnp.arange(half, dtype=F32) / half)
    ang = pos.astype(F32)[:, None] * inv[None]
    cos = jnp.tile(jnp.cos(ang), (1, LANES // half))
    sin = jnp.sin(ang)
    sin_signed = jnp.tile(jnp.concatenate([-sin, sin], axis=1), (1, LANES // ATTN_DH))
    return cos, sin_signed


def _pad_lanes(v, width):
    return jnp.pad(v, (0, width - v.shape[0]))


def _rows8(rows_list, width):
    out = jnp.zeros((8, width), F32)
    for r, v in enumerate(rows_list):
        out = out.at[r].set(v)
    return out


def _make_group(nb, seq_len, pos0, mod_rows, tm):
    m = nb * seq_len
    if seq_len >= TILE_ROWS:
        nseq, lseq = 1, TILE_ROWS
        no, nt = nb, seq_len // TILE_ROWS
    else:
        nseq, lseq = TILE_ROWS // seq_len, seq_len
        no, nt = m // TILE_ROWS, 1
    per_row = mod_rows.shape[-2] != 1

    def mod_spec(l, kind, tile, row_tile=lambda i, *_: i):
        if per_row:
            def imap(*idx):
                return (l, kind, row_tile(*idx), 0)
            return pl.BlockSpec((None, None, tile, D_MODEL), imap)

        def imap(*idx):
            return (l, (row_tile(*idx) * tile) // seq_len * N_MOD + kind, 0, 0)
        return pl.BlockSpec((None, None, 1, D_MODEL), imap)

    return dict(nb=nb, seq=seq_len, m=m, tm=tm, nseq=nseq, lseq=lseq, no=no, nt=nt, pos0=pos0, mod_spec=mod_spec)


def _trunk(grp, x, mod, states, cos, sin, wts, mlp_w):
    w_in_b, w_tail, w_out_b, w_up, w_down, norm1_g, norm2_g, mix_pars = wts
    outs = [[] for _ in range(N_STATES)]
    for l in range(DEPTH):
        if grp["nseq"] == 1 and states is None:
            x3 = x.reshape(grp["nb"], grp["seq"], D_MODEL)
            x3, h2, *new_states = _front_call(grp, l, x3, norm1_g, norm2_g, mod, w_in_b, w_tail, w_out_b, cos, sin,
                                              mix_pars)
            x, h2 = x3.reshape(grp["m"], D_MODEL), h2.reshape(grp["m"], D_MODEL)
        else:
            proj, tail = _inproj_call(grp, l, x, norm1_g, mod, w_in_b, w_tail)
            x, h2, *new_states = _mix_call(grp, l, x, proj, tail, norm2_g, mod, w_out_b, cos, sin, mix_pars, states)
        if l + 1 < DEPTH and mlp_w[l + 1] is None:
            x, nu, nd = _mlp_call(grp, l, h2, x, mod, *mlp_w[l], w_up, w_down)
            mlp_w[l + 1] = [nu, nd]
        else:
            x, = _mlp_call(grp, l, h2, x, mod, *mlp_w[l])
        for lst, val in zip(outs, new_states):
            lst.append(val)
    return x, [jnp.stack(o) for o in outs]


def _ssm_state_to_kernel(s):
    lead = s.shape[:-3]
    r = SSM_HEADS // SSM_GROUPS
    s = s.reshape(lead + (SSM_GROUPS, r, SSM_P, SSM_N))
    s = jnp.moveaxis(s, -1, -3)
    return s.reshape(lead + (SSM_GROUPS, SSM_N, r * SSM_P))


def _ssm_state_from_kernel(s):
    lead = s.shape[:-3]
    r = SSM_HEADS // SSM_GROUPS
    s = s.reshape(lead + (SSM_GROUPS, SSM_N, r, SSM_P))
    s = jnp.moveaxis(s, -3, -1)
    return s.reshape(lead + (SSM_HEADS, SSM_P, SSM_N))


def kernel(x_prompt, x_sample, c_prompt, c_sample, state_hgrn, state_pool, state_ssm, state_conv, cache_k, cache_v, norm1_g, norm2_g, w_ada, b_ada, w_in, hgrn_lb_logits, hgrn_norm_g, pool_w, pool_scale, conv_w, conv_b, dt_bias, a_log, d_skip, ssm_norm_g, q_norm_g, k_norm_g, sinks, w_out, w_up, w_down):
    bp, seq, _ = x_prompt.shape
    bs, dseq, _ = x_sample.shape

    w_in_b = w_in.astype(BF16)
    dt_end = MAIN_WIDTH + SSM_HEADS
    w_tail = jnp.concatenate(
        [w_in_b[:, :, dt_end:], w_in_b[:, :, MAIN_WIDTH:dt_end],
         jnp.zeros((DEPTH, D_MODEL, LANES - SSM_HEADS), BF16)], axis=-1)
    w_out_b = w_out.astype(BF16)
    pool_w_b = pool_w.astype(BF16)

    p = jax.nn.softmax(hgrn_lb_logits.astype(F32), axis=0)
    cs = jnp.cumsum(p, axis=0)
    lbs = cs - cs[:1]
    hgrn_par = jnp.stack([_rows8([jnp.log(lbs[l]), jnp.log1p(-lbs[l]), 1.0 - lbs[l], hgrn_norm_g[l], lbs[l]], W_GROUP)
                          for l in range(DEPTH)])
    ssd_par = jnp.stack([_rows8([_pad_lanes(dt_bias[l], LANES), _pad_lanes(a_log[l], LANES)], LANES)
                         for l in range(DEPTH)])
    ssd_par5 = jnp.stack([_rows8([jnp.repeat(d_skip[l], SSM_P), ssm_norm_g[l]], W_GROUP) for l in range(DEPTH)])
    swa_par = jnp.stack([_rows8([jnp.tile(q_norm_g[l], 2), jnp.tile(k_norm_g[l], 2)], LANES) for l in range(DEPTH)])
    sink_par = jnp.broadcast_to(sinks[:, :, None], (DEPTH, ATTN_HEADS, LANES))
    mix_pars = (hgrn_par, pool_w_b, pool_scale.reshape(DEPTH, 1, W_GROUP), conv_w,
                conv_b.reshape(DEPTH, 1, SSM_CONV_DIM), ssd_par, ssd_par5, swa_par, sink_par)
    wts = (w_in_b, w_tail, w_out_b, w_up, w_down, norm1_g.reshape(DEPTH, 1, D_MODEL),
           norm2_g.reshape(DEPTH, 1, D_MODEL), mix_pars)

    c_all = jnp.concatenate([c_prompt, c_sample], axis=0)
    mod_all = _ada_call(c_all, w_ada, b_ada)
    mod_p = mod_all[:, :bp].reshape(DEPTH, bp * N_MOD, 1, D_MODEL)
    mod_s = mod_all[:, bp:].reshape(DEPTH, bs, N_MOD, D_MODEL)
    mod_s = jnp.repeat(jnp.moveaxis(mod_s, 2, 1), dseq, axis=2)

    grp_p = _make_group(bp, seq, 0, mod_p, 1024)
    grp_s = _make_group(bs, dseq, PAST_LEN, mod_s, bs * dseq)

    cos_p, sin_p = _rope_tables(jnp.arange(seq))
    cos_s, sin_s = _rope_tables(PAST_LEN + (jnp.arange(TILE_ROWS) % dseq))

    mlp_w = [[w_up[0].astype(BF16), w_down[0].astype(BF16)]] + [None] * (DEPTH - 1)
    y_p, st_p = _trunk(grp_p, x_prompt.reshape(bp * seq, D_MODEL), mod_p, None, cos_p, sin_p, wts, mlp_w)

    kv_flat = (DEPTH, bs, WINDOW, ATTN_KV * ATTN_DH)
    states = (state_hgrn, state_pool, state_conv, _ssm_state_to_kernel(state_ssm),
              cache_k.reshape(kv_flat), cache_v.reshape(kv_flat))
    y_s, st_s = _trunk(grp_s, x_sample.reshape(bs * dseq, D_MODEL), mod_s, states, cos_s, sin_s, wts, mlp_w)

    def finish(st, nb):
        n_h, n_p, n_c, n_s, n_k, n_v = st
        return (n_h, n_p, _ssm_state_from_kernel(n_s), n_c,
                n_k.reshape(DEPTH, nb, WINDOW, ATTN_KV, ATTN_DH), n_v.reshape(DEPTH, nb, WINDOW, ATTN_KV, ATTN_DH))

    return ((y_p.reshape(bp, seq, D_MODEL), y_s.reshape(bs, dseq, D_MODEL)) + finish(st_p, bp) + finish(st_s, bs))
```

```python
import functools

import jax
import jax.numpy as jnp
from jax import lax
from jax.experimental import pallas as pl
from jax.experimental.pallas import tpu as pltpu

F32 = jnp.float32
BF16 = jnp.bfloat16

D_MODEL = 2048
DEPTH = 4
PAST_LEN = 16384
W_GROUP = 512
HGRN_HEADS = 4
HGRN_DH = 128
POOL_WINDOWS = (2, 4, 8, 16)
POOL_CH = 128
POOL_BUF = 15
SSM_HEADS = 8
SSM_P = 64
SSM_N = 128
SSM_GROUPS = 2
SSM_CONV = 4
SSM_CONV_DIM = 1024
ATTN_HEADS = 8
ATTN_KV = 2
ATTN_DH = 64
WINDOW = 128
ROPE_THETA = 10000.0
D_FF = 4 * D_MODEL
N_MOD = 6
EPS = 1e-6

LANES = 128
SUBLANES = 8
TILE_ROWS = 128
MLP_ROWS = 512
HIST_PAD = 16
VMEM_LIMIT = 56 * 1024 * 1024

COL_AQ, COL_AF, COL_AI, COL_AG = 0, 512, 1024, 1536
COL_PU, COL_CZ, COL_XBC = 2048, 2560, 3072
MAIN_WIDTH = 4096
COL_DQ, COL_DK, COL_DV, COL_DT = 0, 512, 640, 768
TAIL_WIDTH = 896
NEG_BIG = -1e30

_NT = (((1,), (1,)), ((), ()))
_TN = (((0,), (0,)), ((), ()))


def _dot(a, b):
    return jnp.dot(a.astype(BF16), b.astype(BF16), preferred_element_type=F32)


def _dot_nt(a, b):
    return lax.dot_general(a.astype(BF16), b.astype(BF16), _NT, preferred_element_type=F32)


def _dot_tn(a, b):
    return lax.dot_general(a.astype(BF16), b.astype(BF16), _TN, preferred_element_type=F32)


def _sigmoid(x):
    return 1.0 / (1.0 + jnp.exp(-x))


def _cparams(sem):
    return pltpu.CompilerParams(dimension_semantics=sem, vmem_limit_bytes=VMEM_LIMIT)


def _ada_kernel(c_ref, w_ref, b_ref, o_ref):
    c = c_ref[...]
    s = c * _sigmoid(c)
    o_ref[0] = _dot(s, w_ref[0]) + b_ref[0]


def _ada_call(c_all, w_ada, b_ada):
    rows = c_all.shape[0]
    n = w_ada.shape[-1]
    tn = 1024
    return pl.pallas_call(
        _ada_kernel,
        grid=(DEPTH, n // tn),
        in_specs=[
            pl.BlockSpec((rows, D_MODEL), lambda l, j: (0, 0)),
            pl.BlockSpec((1, D_MODEL, tn), lambda l, j: (l, 0, j)),
            pl.BlockSpec((1, 1, tn), lambda l, j: (l, 0, j)),
        ],
        out_specs=pl.BlockSpec((1, rows, tn), lambda l, j: (l, 0, j)),
        out_shape=jax.ShapeDtypeStruct((DEPTH, rows, n), F32),
        compiler_params=_cparams(("arbitrary", "arbitrary")),
        name="ada_mod",
    )(c_all, w_ada, b_ada.reshape(DEPTH, 1, n))


def _modulated_norm(x, g, scale, shift):
    ms = jnp.mean(x * x, axis=-1, keepdims=True)
    y = x * lax.rsqrt(ms + EPS) * g
    return y * (1.0 + scale) + shift


def _layer_spec(l, shape, **kw):
    zeros = (0,) * len(shape)
    return pl.BlockSpec((None,) + tuple(shape), lambda *_: (l,) + zeros, **kw)


def _inproj_kernel(x_ref, g_ref, sc_ref, sh_ref, wm_ref, wt_ref, om_ref, ot_ref, h_ref):
    j = pl.program_id(1)
    n_main = pl.num_programs(1) - 1

    @pl.when(j == 0)
    def _():
        h_ref[...] = _modulated_norm(x_ref[...], g_ref[...], sc_ref[...], sh_ref[...]).astype(BF16)

    @pl.when(j < n_main)
    def _():
        om_ref[...] = jnp.dot(h_ref[...], wm_ref[...], preferred_element_type=F32)

    @pl.when(j == n_main)
    def _():
        ot_ref[...] = jnp.dot(h_ref[...], wt_ref[...], preferred_element_type=F32)


def _inproj_call(grp, l, x, g, mod, w_main, w_tail):
    m, tm = grp["m"], grp["tm"]
    tn = 1024
    n_main = MAIN_WIDTH // tn
    return pl.pallas_call(
        _inproj_kernel,
        grid=(m // tm, n_main + 1),
        in_specs=[
            pl.BlockSpec((tm, D_MODEL), lambda i, j: (i, 0)),
            _layer_spec(l, (1, D_MODEL)),
            grp["mod_spec"](l, 1, tm),
            grp["mod_spec"](l, 0, tm),
            pl.BlockSpec((None, D_MODEL, tn), lambda i, j: (0, 0, jnp.minimum(j, n_main - 1))),
            _layer_spec(0, (D_MODEL, TAIL_WIDTH)),
        ],
        out_specs=[pl.BlockSpec((tm, tn), lambda i, j: (i, jnp.minimum(j, n_main - 1))),
                   pl.BlockSpec((tm, TAIL_WIDTH), lambda i, j: (i, 0))],
        out_shape=[jax.ShapeDtypeStruct((m, MAIN_WIDTH), F32), jax.ShapeDtypeStruct((m, TAIL_WIDTH), F32)],
        scratch_shapes=[pltpu.VMEM((tm, D_MODEL), BF16)],
        compiler_params=_cparams(("arbitrary", "arbitrary")),
        name="in_proj",
    )(x, g, mod, mod, w_main, w_tail)


def _mlp_kernel(*refs, n_cast):
    h_ref, x_ref, gate_ref, wu_ref, wd_ref = refs[:5]
    src_refs = refs[5:5 + n_cast]
    o_ref = refs[5 + n_cast]
    dst_refs = refs[6 + n_cast:]
    j = pl.program_id(1)

    @pl.when(j == 0)
    def _():
        o_ref[...] = jnp.zeros_like(o_ref)

    u = jnp.dot(h_ref[...], wu_ref[0], preferred_element_type=F32)
    a = jnp.square(jnp.maximum(u, 0.0))
    o_ref[...] += jnp.dot(a.astype(BF16), wd_ref[0], preferred_element_type=F32)
    for src, dst in zip(src_refs, dst_refs):
        dst[...] = src[...].astype(BF16)

    @pl.when(j == pl.num_programs(1) - 1)
    def _():
        o_ref[...] = x_ref[...] + gate_ref[...] * o_ref[...]


def _mlp_call(grp, l, h2, x, mod, wu_b, wd_b, next_f32=()):
    m, tm = grp["m"], grp["tm_mlp"]
    tf = 1024
    gi, gj = m // tm, D_FF // tf
    in_specs = [
        pl.BlockSpec((tm, D_MODEL), lambda i, j: (i, 0)),
        pl.BlockSpec((tm, D_MODEL), lambda i, j: (i, 0), pipeline_mode=pl.Buffered(1)),
        grp["mod_spec"](l, 5, tm),
        pl.BlockSpec((1, D_MODEL, tf), lambda i, j: (0, 0, j)),
        pl.BlockSpec((1, tf, D_MODEL), lambda i, j: (0, j, 0)),
    ]
    args = [h2, x, mod, wu_b, wd_b]
    out_specs = [pl.BlockSpec((tm, D_MODEL), lambda i, j: (i, 0))]
    out_shape = [jax.ShapeDtypeStruct((m, D_MODEL), F32)]
    for w in next_f32:
        rows, width = w.shape[1] // (gi * gj), w.shape[2]
        in_specs.append(pl.BlockSpec((1, rows, width), lambda i, j: (l + 1, i * gj + j, 0)))
        out_specs.append(pl.BlockSpec((1, rows, width), lambda i, j: (0, i * gj + j, 0)))
        out_shape.append(jax.ShapeDtypeStruct((1,) + w.shape[1:], BF16))
        args.append(w)
    return pl.pallas_call(
        functools.partial(_mlp_kernel, n_cast=len(next_f32)),
        grid=(gi, gj),
        in_specs=in_specs,
        out_specs=out_specs,
        out_shape=out_shape,
        compiler_params=_cparams(("arbitrary", "arbitrary")),
        name="mlp",
    )(*args)


def _row_iota(shape):
    return lax.broadcasted_iota(jnp.int32, shape, 0)


def _lane_iota(shape):
    return lax.broadcasted_iota(jnp.int32, shape, 1)


def _cat(pieces, axis):
    return pieces[0] if len(pieces) == 1 else jnp.concatenate(pieces, axis=axis)


def _cumsum_rows(x, lseq):
    rows = x.shape[0]
    rr = _row_iota((rows, rows))
    cc = _lane_iota((rows, rows))
    keep = cc <= rr
    if lseq < rows:
        keep = jnp.logical_and(keep, (rr // lseq) == (cc // lseq))
    tri = jnp.where(keep, 1.0, 0.0).astype(BF16)
    hi = x.astype(BF16)
    rem = x - hi.astype(F32)
    mid = rem.astype(BF16)
    lo = (rem - mid.astype(F32)).astype(BF16)
    return (jnp.dot(tri, hi, preferred_element_type=F32) + jnp.dot(tri, mid, preferred_element_type=F32)
            + jnp.dot(tri, lo, preferred_element_type=F32))


def _seq_last_rows(x, nseq, lseq):
    return _cat([jnp.broadcast_to(x[(b + 1) * lseq - 1:(b + 1) * lseq, :], (lseq, x.shape[1]))
                 for b in range(nseq)], 0)


def _hgrn_body(pm_ref, par_ref, st_ref, *, nseq, lseq):
    rows = TILE_ROWS
    dh = HGRN_DH
    aq = pm_ref[:, COL_AQ:COL_AQ + W_GROUP]
    xf = pm_ref[:, COL_AF:COL_AF + W_GROUP]
    v = pm_ref[:, COL_AI:COL_AI + W_GROUP]
    q = aq * _sigmoid(aq)
    e = jnp.exp(-jnp.abs(xf))
    inv = 1.0 / (1.0 + e)
    log_sig = jnp.minimum(xf, 0.0) - jnp.log(1.0 + e)
    log_lb = par_ref[0:1, :]
    bterm = par_ref[1:2, :] + log_sig
    lf = jnp.maximum(log_lb, bterm) + jnp.log(1.0 + jnp.exp(-jnp.abs(log_lb - bterm)))
    one_m_lb = par_ref[2:3, :]
    k = one_m_lb * jnp.where(xf >= 0.0, e * inv, inv)
    f = par_ref[4:5, :] + one_m_lb * jnp.where(xf >= 0.0, inv, e * inv)
    cum = _cumsum_rows(lf, lseq)

    row = _row_iota((rows, W_GROUP))
    heads = [slice(hd * dh, (hd + 1) * dh) for hd in range(HGRN_HEADS)]
    o = [jnp.zeros((rows, dh), F32) for _ in heads]

    sub = min(SUBLANES, lseq)
    fm = jnp.where((row & (sub - 1)) == 0, 0.0, f)
    kg = k
    vs = v
    for d in range(sub):
        if d > 0:
            kg = fm * pltpu.roll(kg, 1, axis=0)
            vs = pltpu.roll(vs, 1, axis=0)
        term = q * kg
        for hd, sl in enumerate(heads):
            o[hd] = o[hd] + jnp.sum(term[:, sl], axis=-1, keepdims=True) * vs[:, sl]

    levels = []
    h = lseq // 2
    while h >= sub:
        levels.append(h)
        h //= 2
    if levels:
        rr = _row_iota((rows, rows))
        cc = _lane_iota((rows, rows))
        p = [jnp.zeros((rows, rows), F32) for _ in heads]
        for h in levels:
            upper = (row & h) != 0
            refm = _cat([jnp.broadcast_to(cum[jb * 2 * h + h - 1:jb * 2 * h + h, :], (2 * h, W_GROUP))
                         for jb in range(rows // (2 * h))], 0)
            x = jnp.exp(jnp.where(upper, cum - refm, refm - cum))
            a_side = jnp.where(upper, q * x, 0.0).astype(BF16)
            b_side = jnp.where(upper, 0.0, k * x).astype(BF16)
            if 2 * h < rows:
                same = (rr // (2 * h)) == (cc // (2 * h))
            for hd, sl in enumerate(heads):
                s = lax.dot_general(a_side[:, sl], b_side[:, sl], _NT, preferred_element_type=F32)
                if 2 * h < rows:
                    s = jnp.where(same, s, 0.0)
                p[hd] = p[hd] + s
        for hd, sl in enumerate(heads):
            o[hd] = o[hd] + _dot(p[hd], v[:, sl])

    qe = (q * jnp.exp(cum)).astype(BF16)
    lastm = _seq_last_rows(cum, nseq, lseq)
    kd = k * jnp.exp(lastm - cum)
    dec = jnp.exp(lastm)
    vb = v.astype(BF16)
    seq_of_row = row // lseq
    for hd, sl in enumerate(heads):
        inter = []
        for b in range(nseq):
            rs = slice(b * lseq, (b + 1) * lseq)
            st = st_ref[b, hd]
            inter.append(lax.dot_general(qe[rs, sl], st.astype(BF16), _NT, preferred_element_type=F32))
            kdb = kd if nseq == 1 else jnp.where(seq_of_row == b, kd, 0.0)
            upd = lax.dot_general(vb[:, sl], kdb[:, sl].astype(BF16), _TN, preferred_element_type=F32)
            st_ref[b, hd] = st * dec[b * lseq:b * lseq + 1, sl] + upd
        o[hd] = o[hd] + _cat(inter, 0)

    gate = _sigmoid(pm_ref[:, COL_AG:COL_AG + W_GROUP])
    out = []
    for hd, sl in enumerate(heads):
        ms = jnp.mean(o[hd] * o[hd], axis=-1, keepdims=True)
        out.append(o[hd] * lax.rsqrt(ms + EPS) * par_ref[3:4, sl] * gate[:, sl])
    return out


def _pool_body(pm_ref, pw_ref, sc_ref, hout_ref, ext, i, *, nseq, lseq, pos0):
    stride = lseq + HIST_PAD
    u = pm_ref[:, COL_PU:COL_PU + W_GROUP]
    for b in range(nseq):
        base = b * stride
        ext[base + HIST_PAD - POOL_BUF:base + HIST_PAD, :] = hout_ref[b]
        ext[base + HIST_PAD:base + HIST_PAD + lseq, :] = u[b * lseq:(b + 1) * lseq, :]

    local = _row_iota((TILE_ROWS, POOL_CH)) & (lseq - 1)
    posn = pos0 + i * lseq + local
    out = []
    for gi, win in enumerate(POOL_WINDOWS):
        cs = slice(gi * POOL_CH, (gi + 1) * POOL_CH)
        pieces = []
        for b in range(nseq):
            base = b * stride + HIST_PAD
            s = ext[base:base + lseq, cs]
            for j in range(1, win):
                s = s + ext[pl.ds(base - j, lseq), cs]
            pieces.append(s)
        cnt = jnp.minimum(posn + 1, win).astype(F32)
        pooled = _cat(pieces, 0) / cnt - u[:, cs]
        out.append(_dot(pooled, pw_ref[gi]) * sc_ref[:, cs])

    for b in range(nseq):
        base = b * stride
        hout_ref[b] = ext[base + lseq + HIST_PAD - POOL_BUF:base + lseq + HIST_PAD, :]
    return out


def _expand_heads(z, emat):
    hi = z.astype(BF16)
    lo = (z - hi.astype(F32)).astype(BF16)
    return (jnp.dot(hi, emat, preferred_element_type=F32) + jnp.dot(lo, emat, preferred_element_type=F32))


def _ssd_body(pm_ref, pt_ref, cw_ref, cb_ref, par_ref, par5_ref, cout_ref, st_ref, ext, *, nseq, lseq):
    rows = TILE_ROWS
    hist = SSM_CONV - 1
    stride = lseq + SUBLANES
    gw = SSM_N

    xbc = pm_ref[:, COL_XBC:COL_XBC + SSM_CONV_DIM]
    for b in range(nseq):
        base = b * stride
        ext[base + SUBLANES - hist:base + SUBLANES, :] = cout_ref[b]
        ext[base + SUBLANES:base + SUBLANES + lseq, :] = xbc[b * lseq:(b + 1) * lseq, :]
    acc = None
    for j in range(SSM_CONV):
        sh = _cat([ext[pl.ds(b * stride + SUBLANES - hist + j, lseq), :] for b in range(nseq)], 0)
        t = sh * cw_ref[j:j + 1, :]
        acc = t if acc is None else acc + t
    for b in range(nseq):
        base = b * stride
        cout_ref[b] = ext[base + lseq + SUBLANES - hist:base + lseq + SUBLANES, :]
    acc = acc + cb_ref[...]
    conv = acc * _sigmoid(acc)
    x = conv[:, 0:W_GROUP]
    bm = conv[:, W_GROUP:W_GROUP + SSM_GROUPS * gw]
    cm = conv[:, W_GROUP + SSM_GROUPS * gw:]

    pre = pt_ref[:, COL_DT:COL_DT + LANES] + par_ref[0:1, :]
    dt = jnp.maximum(pre, 0.0) + jnp.log1p(jnp.exp(-jnp.abs(pre)))
    a = dt * (-jnp.exp(par_ref[1:2, :]))
    cum = _cumsum_rows(a, lseq)
    cum_t = cum.T
    dt_t = dt.T

    rr = _row_iota((rows, rows))
    cc = _lane_iota((rows, rows))
    valid = cc <= rr
    if nseq > 1:
        valid = jnp.logical_and(valid, (rr // lseq) == (cc // lseq))

    erow = _row_iota((LANES, W_GROUP))
    ecol = _lane_iota((LANES, W_GROUP))
    emat = jnp.where(ecol // SSM_P == erow, 1.0, 0.0).astype(BF16)

    lane = _lane_iota((rows, LANES))
    lo_half = lane < SSM_P
    heads_per_group = SSM_HEADS // SSM_GROUPS
    y_chunks = []
    for g in range(SSM_GROUPS):
        gs = slice(g * gw, (g + 1) * gw)
        cb = lax.dot_general(cm[:, gs].astype(BF16), bm[:, gs].astype(BF16), _NT, preferred_element_type=F32)
        for jc in range(heads_per_group // 2):
            chunk = g * (heads_per_group // 2) + jc
            xc = x[:, chunk * LANES:(chunk + 1) * LANES]
            yc = None
            for half in range(2):
                r = 2 * chunk + half
                seg = jnp.broadcast_to(cum[:, r:r + 1], (rows, rows)) - cum_t[r:r + 1, :]
                wts = cb * jnp.exp(jnp.where(valid, seg, NEG_BIG)) * dt_t[r:r + 1, :]
                xm = jnp.where(lo_half, xc, 0.0) if half == 0 else jnp.where(lo_half, 0.0, xc)
                t = _dot(wts, xm)
                yc = t if yc is None else yc + t
            y_chunks.append(yc)
    y = jnp.concatenate(y_chunks, axis=1)

    lastm = _seq_last_rows(cum, nseq, lseq)
    e_cum = _expand_heads(jnp.exp(cum), emat)
    e_wst = _expand_heads(dt * jnp.exp(lastm - cum), emat)
    e_dec = _expand_heads(jnp.exp(lastm), emat)
    xs = x * e_wst
    row5 = _row_iota((rows, W_GROUP))
    inter_groups = []
    pw = heads_per_group * SSM_P
    for g in range(SSM_GROUPS):
        gs = slice(g * gw, (g + 1) * gw)
        ps = slice(g * pw, (g + 1) * pw)
        inter = []
        for b in range(nseq):
            rs = slice(b * lseq, (b + 1) * lseq)
            st = st_ref[b, g]
            inter.append(_dot(cm[rs, gs], st))
            xsb = xs if nseq == 1 else jnp.where(row5 // lseq == b, xs, 0.0)
            upd = _dot_tn(bm[:, gs], xsb[:, ps])
            st_ref[b, g] = st * e_dec[b * lseq:b * lseq + 1, ps] + upd
        inter_groups.append(_cat(inter, 0))
    y = y + jnp.concatenate(inter_groups, axis=1) * e_cum
    y = y + x * par5_ref[0:1, :]
    z = pm_ref[:, COL_CZ:COL_CZ + W_GROUP]
    y = y * (z * _sigmoid(z))
    ms = jnp.mean(y * y, axis=-1, keepdims=True)
    return y * lax.rsqrt(ms + EPS) * par5_ref[1:2, :]


def _head_rmsnorm(x, g, lo_half):
    sq = x * x
    s_lo = jnp.sum(jnp.where(lo_half, sq, 0.0), axis=-1, keepdims=True)
    s_hi = jnp.sum(jnp.where(lo_half, 0.0, sq), axis=-1, keepdims=True)
    ms = jnp.where(lo_half, s_lo, s_hi) * (1.0 / ATTN_DH)
    return x * lax.rsqrt(ms + EPS) * g


def _rope(x, cos, sin_signed, upper_half):
    partner = jnp.where(upper_half, pltpu.roll(x, ATTN_DH // 2, axis=1), pltpu.roll(x, LANES - ATTN_DH // 2, axis=1))
    return x * cos + partner * sin_signed


def _dup_head(x, g, lane):
    own = jnp.where((lane // ATTN_DH) == g, x, 0.0)
    return own + pltpu.roll(own, ATTN_DH, axis=1)


def _swa_body(pt_ref, cos_ref, sin_ref, par_ref, sink_ref, kout_ref, vout_ref, i, *, nseq, lseq, has_state):
    rows = TILE_ROWS
    rep = ATTN_HEADS // ATTN_KV
    lane = _lane_iota((rows, LANES))
    lo_half = lane < ATTN_DH
    upper_half = (lane & (ATTN_DH // 2)) != 0
    cos = cos_ref[...]
    sin = sin_ref[...]
    kn = _rope(_head_rmsnorm(pt_ref[:, COL_DK:COL_DK + LANES], par_ref[1:2, :], lo_half), cos, sin, upper_half)
    vn = pt_ref[:, COL_DV:COL_DV + LANES]
    scale = ATTN_DH ** -0.5
    qs = []
    for c in range(ATTN_HEADS // 2):
        qc = pt_ref[:, COL_DQ + c * LANES:COL_DQ + (c + 1) * LANES]
        qs.append(_rope(_head_rmsnorm(qc, par_ref[0:1, :], lo_half), cos, sin, upper_half) * scale)

    mq = rep * lseq
    qrow = _row_iota((mq, LANES))
    q_local = qrow & (lseq - 1)
    kcol = _lane_iota((mq, LANES))
    hist_ok = jnp.logical_or(has_state, i > 0)
    hist_valid = jnp.logical_and(kcol > q_local, hist_ok)
    lane_w = _lane_iota((WINDOW, LANES))

    out_rows = [[None] * nseq for _ in range(ATTN_HEADS // 2)]
    for g in range(ATTN_KV):
        k_new = _dup_head(kn, g, lane).astype(BF16)
        v_new = _dup_head(vn, g, lane).astype(BF16)
        for b in range(nseq):
            rs = slice(b * lseq, (b + 1) * lseq)
            q4 = []
            sink_rows = []
            for r in range(rep):
                hidx = g * rep + r
                qc = qs[hidx // 2][rs, :]
                lo_l = lo_half[0:lseq, :]
                q4.append(jnp.where(lo_l, qc, 0.0) if hidx % 2 == 0 else jnp.where(lo_l, 0.0, qc))
                sink_rows.append(jnp.broadcast_to(sink_ref[hidx:hidx + 1, 0:1], (lseq, 1)))
            q4 = jnp.concatenate(q4, axis=0).astype(BF16)
            sink = jnp.concatenate(sink_rows, axis=0)
            k_hist = _dup_head(kout_ref[b], g, lane_w).astype(BF16)
            v_hist = _dup_head(vout_ref[b], g, lane_w).astype(BF16)
            s_h = lax.dot_general(q4, k_hist, _NT, preferred_element_type=F32)
            s_n = lax.dot_general(q4, k_new, _NT, preferred_element_type=F32)
            s_h = jnp.where(hist_valid, s_h, NEG_BIG)
            new_valid = jnp.logical_and(kcol // lseq == b, (kcol & (lseq - 1)) <= q_local)
            s_n = jnp.where(new_valid, s_n, NEG_BIG)
            m = jnp.maximum(jnp.maximum(jnp.max(s_h, axis=-1, keepdims=True),
                                        jnp.max(s_n, axis=-1, keepdims=True)), sink)
            p_h = jnp.exp(s_h - m)
            p_n = jnp.exp(s_n - m)
            den = (jnp.sum(p_h, axis=-1, keepdims=True) + jnp.sum(p_n, axis=-1, keepdims=True)
                   + jnp.exp(sink - m))
            o4 = (jnp.dot(p_h.astype(BF16), v_hist, preferred_element_type=F32)
                  + jnp.dot(p_n.astype(BF16), v_new, preferred_element_type=F32)) / den
            for pair in range(rep // 2):
                c = (g * rep) // 2 + pair
                o_lo = o4[(2 * pair) * lseq:(2 * pair + 1) * lseq, :]
                o_hi = o4[(2 * pair + 1) * lseq:(2 * pair + 2) * lseq, :]
                out_rows[c][b] = jnp.where(lo_half[0:lseq, :], o_lo, o_hi)

    for b in range(nseq):
        rs = slice(b * lseq, (b + 1) * lseq)
        if lseq < WINDOW:
            keep_k = kout_ref[b, lseq:WINDOW, :]
            keep_v = vout_ref[b, lseq:WINDOW, :]
            kout_ref[b, 0:WINDOW - lseq, :] = keep_k
            vout_ref[b, 0:WINDOW - lseq, :] = keep_v
        kout_ref[b, WINDOW - lseq:WINDOW, :] = kn[rs, :]
        vout_ref[b, WINDOW - lseq:WINDOW, :] = vn[rs, :]
    return [_cat(out_rows[c], 0) for c in range(ATTN_HEADS // 2)]


N_STATES = 6


def _transpose_heads(st_ref, n):
    for b in range(n):
        for hd in range(HGRN_HEADS):
            st_ref[b, hd] = st_ref[b, hd].T


def _mix_kernel(*refs, nseq, lseq, has_state, pos0):
    (pm_ref, pt_ref, x_ref, gate_ref, g2_ref, sc2_ref, sh2_ref, w_ref, cos_ref, sin_ref, hpar_ref, pw_ref, psc_ref,
     cw_ref, cb_ref, spar_ref, spar5_ref, apar_ref, sink_ref) = refs[:19]
    rest = refs[19:]
    if has_state:
        init_refs, rest = rest[:N_STATES], rest[N_STATES:]
    else:
        init_refs = (None,) * N_STATES
    o_ref, h2_ref = rest[:2]
    state_refs = rest[2:2 + N_STATES]
    pool_ext, conv_ext = rest[2 + N_STATES:]
    nh_ref, np_ref, nc_ref, ns_ref, nk_ref, nv_ref = state_refs
    i = pl.program_id(1)

    @pl.when(i == 0)
    def _():
        for dst, src in zip(state_refs, init_refs):
            if src is None:
                dst[...] = jnp.zeros_like(dst)
            elif dst is nh_ref:
                for b in range(nseq):
                    for hd in range(HGRN_HEADS):
                        dst[b, hd] = src[b, hd].T
            else:
                dst[...] = src[...]

    ya = _hgrn_body(pm_ref, hpar_ref, nh_ref, nseq=nseq, lseq=lseq)
    yb = _pool_body(pm_ref, pw_ref, psc_ref, np_ref, pool_ext, i, nseq=nseq, lseq=lseq, pos0=pos0)
    yc = _ssd_body(pm_ref, pt_ref, cw_ref, cb_ref, spar_ref, spar5_ref, nc_ref, ns_ref, conv_ext, nseq=nseq, lseq=lseq)
    yd = _swa_body(pt_ref, cos_ref, sin_ref, apar_ref, sink_ref, nk_ref, nv_ref, i, nseq=nseq, lseq=lseq,
                   has_state=has_state)
    y = jnp.concatenate([t.astype(BF16) for t in ya + yb + [yc] + yd], axis=1)
    mixed = jnp.dot(y, w_ref[...], preferred_element_type=F32)
    x1 = x_ref[...] + gate_ref[...] * mixed
    o_ref[...] = x1
    h2_ref[...] = _modulated_norm(x1, g2_ref[...], sc2_ref[...], sh2_ref[...]).astype(BF16)

    @pl.when(i == pl.num_programs(1) - 1)
    def _():
        _transpose_heads(nh_ref, nseq)


def _mix_call(grp, l, x, proj, tail, g2, mod, w_out, cos, sin, pars, states):
    nseq, lseq, no, nt = grp["nseq"], grp["lseq"], grp["no"], grp["nt"]
    has_state = states is not None
    hpar, pool_w, pool_sc, conv_w, conv_b, spar, spar5, apar, sink = pars
    tails = [(HGRN_HEADS, HGRN_DH, HGRN_DH), (POOL_BUF, W_GROUP), (SSM_CONV - 1, SSM_CONV_DIM),
             (SSM_GROUPS, SSM_N, (SSM_HEADS // SSM_GROUPS) * SSM_P),
             (WINDOW, ATTN_KV * ATTN_DH), (WINDOW, ATTN_KV * ATTN_DH)]

    def row_spec(width):
        return pl.BlockSpec((TILE_ROWS, width), lambda o, i: (o * nt + i, 0))

    in_specs = [
        row_spec(MAIN_WIDTH), row_spec(TAIL_WIDTH), row_spec(D_MODEL),
        grp["mod_spec"](l, 2, TILE_ROWS, lambda o, i: o * nt + i),
        _layer_spec(l, (1, D_MODEL)),
        grp["mod_spec"](l, 4, TILE_ROWS, lambda o, i: o * nt + i),
        grp["mod_spec"](l, 3, TILE_ROWS, lambda o, i: o * nt + i),
        _layer_spec(0, (D_MODEL, D_MODEL), pipeline_mode=pl.Buffered(1)),
        pl.BlockSpec((TILE_ROWS, LANES), lambda o, i: (i, 0)),
        pl.BlockSpec((TILE_ROWS, LANES), lambda o, i: (i, 0)),
        _layer_spec(l, (8, W_GROUP)), _layer_spec(l, (len(POOL_WINDOWS), POOL_CH, POOL_CH)),
        _layer_spec(l, (1, W_GROUP)), _layer_spec(l, (SSM_CONV, SSM_CONV_DIM)), _layer_spec(l, (1, SSM_CONV_DIM)),
        _layer_spec(l, (8, LANES)), _layer_spec(l, (8, W_GROUP)), _layer_spec(l, (8, LANES)), _layer_spec(l, (8, LANES)),
    ]
    args = [proj, tail, x, mod, g2, mod, mod, w_out, cos, sin, hpar, pool_w, pool_sc, conv_w, conv_b, spar, spar5,
            apar, sink]
    if has_state:
        for t in tails:
            zeros = (0,) * len(t)
            in_specs.append(pl.BlockSpec((None, nseq) + t, lambda o, i, zeros=zeros: (l, o) + zeros,
                                         pipeline_mode=pl.Buffered(1)))
        args += list(states)
    out_specs = [row_spec(D_MODEL), row_spec(D_MODEL)]
    out_shape = [jax.ShapeDtypeStruct((grp["m"], D_MODEL), F32), jax.ShapeDtypeStruct((grp["m"], D_MODEL), BF16)]
    state_mode = dict(pipeline_mode=pl.Buffered(1)) if nseq > 1 else {}
    for t in tails:
        zeros = (0,) * len(t)
        out_specs.append(pl.BlockSpec((nseq,) + t, lambda o, i, zeros=zeros: (o,) + zeros, **state_mode))
        out_shape.append(jax.ShapeDtypeStruct((grp["nb"],) + t, F32))
    return pl.pallas_call(
        functools.partial(_mix_kernel, nseq=nseq, lseq=lseq, has_state=has_state, pos0=grp["pos0"]),
        grid=(no, nt),
        in_specs=in_specs,
        out_specs=out_specs,
        out_shape=out_shape,
        scratch_shapes=[pltpu.VMEM((nseq * (lseq + HIST_PAD), W_GROUP), F32),
                        pltpu.VMEM((nseq * (lseq + SUBLANES), SSM_CONV_DIM), F32)],
        compiler_params=_cparams(("arbitrary", "arbitrary")),
        name="mix",
    )(*args)


def _front_kernel(*refs, nsub, pos0):
    (x_ref, g_ref, g2_ref, mod_ref, wm_ref, wt_ref, w_ref, cos_ref, sin_ref, hpar_ref, pw_ref, psc_ref, cw_ref,
     cb_ref, spar_ref, spar5_ref, apar_ref, sink_ref) = refs[:18]
    o_ref, h2_ref = refs[18:20]
    state_refs = refs[20:20 + N_STATES]
    pm_scr, pt_scr = refs[20 + N_STATES:22 + N_STATES]
    ext_refs = refs[22 + N_STATES:]
    rows = TILE_ROWS
    i = pl.program_id(0)

    @pl.when(i == 0)
    def _():
        for dst in state_refs:
            dst[...] = jnp.zeros_like(dst)

    h = _cat([_modulated_norm(x_ref[s], g_ref[...], mod_ref[s * N_MOD + 1], mod_ref[s * N_MOD + 0]).astype(BF16)
              for s in range(nsub)], 0)
    pm_scr[...] = jnp.dot(h, wm_ref[...], preferred_element_type=F32)
    pt_scr[...] = jnp.dot(h, wt_ref[...], preferred_element_type=F32)

    ys = []
    for s in range(nsub):
        pm = pm_scr.at[pl.ds(s * rows, rows)]
        pt = pt_scr.at[pl.ds(s * rows, rows)]
        nh, npool, nc, ns, nk, nv = (r.at[pl.ds(s, 1)] for r in state_refs)
        ya = _hgrn_body(pm, hpar_ref, nh, nseq=1, lseq=rows)
        yb = _pool_body(pm, pw_ref, psc_ref, npool, ext_refs[2 * s], i, nseq=1, lseq=rows, pos0=pos0)
        yc = _ssd_body(pm, pt, cw_ref, cb_ref, spar_ref, spar5_ref, nc, ns, ext_refs[2 * s + 1], nseq=1, lseq=rows)
        yd = _swa_body(pt, cos_ref, sin_ref, apar_ref, sink_ref, nk, nv, i, nseq=1, lseq=rows, has_state=False)
        ys.append(jnp.concatenate([t.astype(BF16) for t in ya + yb + [yc] + yd], axis=1))
    mixed = jnp.dot(_cat(ys, 0), w_ref[...], preferred_element_type=F32)
    for s in range(nsub):
        x1 = x_ref[s] + mod_ref[s * N_MOD + 2] * mixed[s * rows:(s + 1) * rows, :]
        o_ref[s] = x1
        h2_ref[s] = _modulated_norm(x1, g2_ref[...], mod_ref[s * N_MOD + 4], mod_ref[s * N_MOD + 3]).astype(BF16)

    @pl.when(i == pl.num_programs(0) - 1)
    def _():
        _transpose_heads(state_refs[0], nsub)


def _front_call(grp, l, x, g, g2, mod, w_in_b, w_tail, w_out, cos, sin, pars):
    nb, nt = grp["nb"], grp["nt"]
    hpar, pool_w, pool_sc, conv_w, conv_b, spar, spar5, apar, sink = pars
    tails = [(HGRN_HEADS, HGRN_DH, HGRN_DH), (POOL_BUF, W_GROUP), (SSM_CONV - 1, SSM_CONV_DIM),
             (SSM_GROUPS, SSM_N, (SSM_HEADS // SSM_GROUPS) * SSM_P),
             (WINDOW, ATTN_KV * ATTN_DH), (WINDOW, ATTN_KV * ATTN_DH)]
    once = dict(pipeline_mode=pl.Buffered(1))
    x_spec = pl.BlockSpec((nb, TILE_ROWS, D_MODEL), lambda i: (0, i, 0))
    in_specs = [
        x_spec, _layer_spec(l, (1, D_MODEL)), _layer_spec(l, (1, D_MODEL)), _layer_spec(l, (nb * N_MOD, 1, D_MODEL)),
        _layer_spec(0, (D_MODEL, MAIN_WIDTH), **once), _layer_spec(0, (D_MODEL, TAIL_WIDTH), **once),
        _layer_spec(0, (D_MODEL, D_MODEL), **once),
        pl.BlockSpec((TILE_ROWS, LANES), lambda i: (i, 0)), pl.BlockSpec((TILE_ROWS, LANES), lambda i: (i, 0)),
        _layer_spec(l, (8, W_GROUP)), _layer_spec(l, (len(POOL_WINDOWS), POOL_CH, POOL_CH)),
        _layer_spec(l, (1, W_GROUP)), _layer_spec(l, (SSM_CONV, SSM_CONV_DIM)), _layer_spec(l, (1, SSM_CONV_DIM)),
        _layer_spec(l, (8, LANES)), _layer_spec(l, (8, W_GROUP)), _layer_spec(l, (8, LANES)), _layer_spec(l, (8, LANES)),
    ]
    args = [x, g, g2, mod, w_in_b, w_tail, w_out, cos, sin, hpar, pool_w, pool_sc, conv_w, conv_b, spar, spar5, apar, sink]
    out_specs = [x_spec, x_spec]
    out_shape = [jax.ShapeDtypeStruct(x.shape, F32), jax.ShapeDtypeStruct(x.shape, BF16)]
    for t in tails:
        zeros = (0,) * (len(t) + 1)
        out_specs.append(pl.BlockSpec((nb,) + t, lambda i, zeros=zeros: zeros))
        out_shape.append(jax.ShapeDtypeStruct((nb,) + t, F32))
    scratch = [pltpu.VMEM((nb * TILE_ROWS, MAIN_WIDTH), F32), pltpu.VMEM((nb * TILE_ROWS, TAIL_WIDTH), F32)]
    for _ in range(nb):
        scratch += [pltpu.VMEM((TILE_ROWS + HIST_PAD, W_GROUP), F32),
                    pltpu.VMEM((TILE_ROWS + SUBLANES, SSM_CONV_DIM), F32)]
    return pl.pallas_call(
        functools.partial(_front_kernel, nsub=nb, pos0=grp["pos0"]),
        grid=(nt,),
        in_specs=in_specs,
        out_specs=out_specs,
        out_shape=out_shape,
        scratch_shapes=scratch,
        compiler_params=_cparams(("arbitrary",)),
        name="front",
    )(*args)


def _rope_tables(pos):
    half = ATTN_DH // 2
    inv = ROPE_THETA ** (-jnp.arange(half, dtype=F32) / half)
    ang = pos.astype(F32)[:, None] * inv[None]
    cos = jnp.tile(jnp.cos(ang), (1, LANES // half))
    sin = jnp.sin(ang)
    sin_signed = jnp.tile(jnp.concatenate([-sin, sin], axis=1), (1, LANES // ATTN_DH))
    return cos, sin_signed


def _pad_lanes(v, width):
    return jnp.pad(v, (0, width - v.shape[0]))


def _rows8(rows_list, width):
    out = jnp.zeros((8, width), F32)
    for r, v in enumerate(rows_list):
        out = out.at[r].set(v)
    return out


def _make_group(nb, seq_len, pos0, mod_rows, tm):
    m = nb * seq_len
    if seq_len >= TILE_ROWS:
        nseq, lseq = 1, TILE_ROWS
        no, nt = nb, seq_len // TILE_ROWS
    else:
        nseq, lseq = TILE_ROWS // seq_len, seq_len
        no, nt = m // TILE_ROWS, 1
    per_row = mod_rows.shape[-2] != 1

    def mod_spec(l, kind, tile, row_tile=lambda i, *_: i):
        if per_row:
            def imap(*idx):
                return (l, kind, row_tile(*idx), 0)
            return pl.BlockSpec((None, None, tile, D_MODEL), imap)

        def imap(*idx):
            return (l, (row_tile(*idx) * tile) // seq_len * N_MOD + kind, 0, 0)
        return pl.BlockSpec((None, None, 1, D_MODEL), imap)

    return dict(nb=nb, seq=seq_len, m=m, tm=tm, tm_mlp=min(m, MLP_ROWS), nseq=nseq, lseq=lseq, no=no, nt=nt,
                pos0=pos0, mod_spec=mod_spec)


def _tail_weights(w_in_b):
    dt_end = MAIN_WIDTH + SSM_HEADS
    return jnp.concatenate([w_in_b[:, :, dt_end:], w_in_b[:, :, MAIN_WIDTH:dt_end],
                            jnp.zeros((1, D_MODEL, LANES - SSM_HEADS), BF16)], axis=-1)


def _trunk(grp, x, mod, states, cos, sin, wts, bf16_w):
    f32_w, norm1_g, norm2_g, mix_pars = wts
    outs = [[] for _ in range(N_STATES)]
    for l in range(DEPTH):
        w_in_b, w_out_b, wu_b, wd_b = bf16_w[l]
        w_tail = _tail_weights(w_in_b)
        if grp["nseq"] == 1 and states is None:
            x3 = x.reshape(grp["nb"], grp["seq"], D_MODEL)
            x3, h2, *new_states = _front_call(grp, l, x3, norm1_g, norm2_g, mod, w_in_b, w_tail, w_out_b, cos, sin,
                                              mix_pars)
            x, h2 = x3.reshape(grp["m"], D_MODEL), h2.reshape(grp["m"], D_MODEL)
        else:
            proj, tail = _inproj_call(grp, l, x, norm1_g, mod, w_in_b, w_tail)
            x, h2, *new_states = _mix_call(grp, l, x, proj, tail, norm2_g, mod, w_out_b, cos, sin, mix_pars, states)
        if l + 1 < DEPTH and bf16_w[l + 1] is None:
            x, *converted = _mlp_call(grp, l, h2, x, mod, wu_b, wd_b, f32_w)
            bf16_w[l + 1] = converted
        else:
            x, = _mlp_call(grp, l, h2, x, mod, wu_b, wd_b)
        for lst, val in zip(outs, new_states):
            lst.append(val)
    return x, [jnp.stack(o) for o in outs]


def _ssm_state_to_kernel(s):
    lead = s.shape[:-3]
    r = SSM_HEADS // SSM_GROUPS
    s = s.reshape(lead + (SSM_GROUPS, r, SSM_P, SSM_N))
    s = jnp.moveaxis(s, -1, -3)
    return s.reshape(lead + (SSM_GROUPS, SSM_N, r * SSM_P))


def _ssm_state_from_kernel(s):
    lead = s.shape[:-3]
    r = SSM_HEADS // SSM_GROUPS
    s = s.reshape(lead + (SSM_GROUPS, SSM_N, r, SSM_P))
    s = jnp.moveaxis(s, -3, -1)
    return s.reshape(lead + (SSM_HEADS, SSM_P, SSM_N))


def kernel(x_prompt, x_sample, c_prompt, c_sample, state_hgrn, state_pool, state_ssm, state_conv, cache_k, cache_v, norm1_g, norm2_g, w_ada, b_ada, w_in, hgrn_lb_logits, hgrn_norm_g, pool_w, pool_scale, conv_w, conv_b, dt_bias, a_log, d_skip, ssm_norm_g, q_norm_g, k_norm_g, sinks, w_out, w_up, w_down):
    bp, seq, _ = x_prompt.shape
    bs, dseq, _ = x_sample.shape

    f32_w = (w_in, w_out, w_up, w_down)
    bf16_w = [[w[0:1].astype(BF16) for w in f32_w]] + [None] * (DEPTH - 1)
    pool_w_b = pool_w.astype(BF16)

    p = jax.nn.softmax(hgrn_lb_logits.astype(F32), axis=0)
    cs = jnp.cumsum(p, axis=0)
    lbs = cs - cs[:1]
    hgrn_par = jnp.stack([_rows8([jnp.log(lbs[l]), jnp.log1p(-lbs[l]), 1.0 - lbs[l], hgrn_norm_g[l], lbs[l]], W_GROUP)
                          for l in range(DEPTH)])
    ssd_par = jnp.stack([_rows8([_pad_lanes(dt_bias[l], LANES), _pad_lanes(a_log[l], LANES)], LANES)
                         for l in range(DEPTH)])
    ssd_par5 = jnp.stack([_rows8([jnp.repeat(d_skip[l], SSM_P), ssm_norm_g[l]], W_GROUP) for l in range(DEPTH)])
    swa_par = jnp.stack([_rows8([jnp.tile(q_norm_g[l], 2), jnp.tile(k_norm_g[l], 2)], LANES) for l in range(DEPTH)])
    sink_par = jnp.broadcast_to(sinks[:, :, None], (DEPTH, ATTN_HEADS, LANES))
    mix_pars = (hgrn_par, pool_w_b, pool_scale.reshape(DEPTH, 1, W_GROUP), conv_w,
                conv_b.reshape(DEPTH, 1, SSM_CONV_DIM), ssd_par, ssd_par5, swa_par, sink_par)
    wts = (f32_w, norm1_g.reshape(DEPTH, 1, D_MODEL), norm2_g.reshape(DEPTH, 1, D_MODEL), mix_pars)

    c_all = jnp.concatenate([c_prompt, c_sample], axis=0)
    mod_all = _ada_call(c_all, w_ada, b_ada)
    mod_p = mod_all[:, :bp].reshape(DEPTH, bp * N_MOD, 1, D_MODEL)
    mod_s = mod_all[:, bp:].reshape(DEPTH, bs, N_MOD, D_MODEL)
    mod_s = jnp.repeat(jnp.moveaxis(mod_s, 2, 1), dseq, axis=2)

    grp_p = _make_group(bp, seq, 0, mod_p, 1024)
    grp_s = _make_group(bs, dseq, PAST_LEN, mod_s, bs * dseq)

    cos_p, sin_p = _rope_tables(jnp.arange(seq))
    cos_s, sin_s = _rope_tables(PAST_LEN + (jnp.arange(TILE_ROWS) % dseq))

    y_p, st_p = _trunk(grp_p, x_prompt.reshape(bp * seq, D_MODEL), mod_p, None, cos_p, sin_p, wts, bf16_w)

    kv_flat = (DEPTH, bs, WINDOW, ATTN_KV * ATTN_DH)
    states = (state_hgrn, state_pool, state_conv, _ssm_state_to_kernel(state_ssm),
              cache_k.reshape(kv_flat), cache_v.reshape(kv_flat))
    y_s, st_s = _trunk(grp_s, x_sample.reshape(bs * dseq, D_MODEL), mod_s, states, cos_s, sin_s, wts, bf16_w)

    def finish(st, nb):
        n_h, n_p, n_c, n_s, n_k, n_v = st
        return (n_h, n_p, _ssm_state_from_kernel(n_s), n_c,
                n_k.reshape(DEPTH, nb, WINDOW, ATTN_KV, ATTN_DH), n_v.reshape(DEPTH, nb, WINDOW, ATTN_KV, ATTN_DH))

    return ((y_p.reshape(bp, seq, D_MODEL), y_s.reshape(bs, dseq, D_MODEL)) + finish(st_p, bp) + finish(st_s, bs))
```

```python
import functools

import jax
import jax.numpy as jnp
from jax import lax
from jax.experimental import pallas as pl
from jax.experimental.pallas import tpu as pltpu

F32 = jnp.float32
BF16 = jnp.bfloat16

D_MODEL = 2048
DEPTH = 4
PAST_LEN = 16384
W_GROUP = 512
HGRN_HEADS = 4
HGRN_DH = 128
POOL_WINDOWS = (2, 4, 8, 16)
POOL_CH = 128
POOL_BUF = 15
SSM_HEADS = 8
SSM_P = 64
SSM_N = 128
SSM_GROUPS = 2
SSM_CONV = 4
SSM_CONV_DIM = 1024
ATTN_HEADS = 8
ATTN_KV = 2
ATTN_DH = 64
WINDOW = 128
ROPE_THETA = 10000.0
D_FF = 4 * D_MODEL
N_MOD = 6
EPS = 1e-6

LANES = 128
SUBLANES = 8
TILE_ROWS = 128
MLP_ROWS = 512
HIST_PAD = 16
VMEM_LIMIT = 56 * 1024 * 1024

COL_AQ, COL_AF, COL_AI, COL_AG = 0, 512, 1024, 1536
COL_PU, COL_CZ, COL_XBC = 2048, 2560, 3072
MAIN_WIDTH = 4096
COL_DQ, COL_DK, COL_DV, COL_DT = 0, 512, 640, 768
TAIL_WIDTH = 896
NEG_BIG = -1e30

_NT = (((1,), (1,)), ((), ()))
_TN = (((0,), (0,)), ((), ()))


def _dot(a, b):
    return jnp.dot(a.astype(BF16), b.astype(BF16), preferred_element_type=F32)


def _dot_nt(a, b):
    return lax.dot_general(a.astype(BF16), b.astype(BF16), _NT, preferred_element_type=F32)


def _dot_tn(a, b):
    return lax.dot_general(a.astype(BF16), b.astype(BF16), _TN, preferred_element_type=F32)


def _sigmoid(x):
    return 1.0 / (1.0 + jnp.exp(-x))


def _cparams(sem):
    return pltpu.CompilerParams(dimension_semantics=sem, vmem_limit_bytes=VMEM_LIMIT)


def _ada_kernel(c_ref, w_ref, b_ref, o_ref):
    c = c_ref[...]
    s = c * _sigmoid(c)
    o_ref[0] = _dot(s, w_ref[0]) + b_ref[0]


def _ada_call(c_all, w_ada, b_ada):
    rows = c_all.shape[0]
    n = w_ada.shape[-1]
    tn = 1024
    return pl.pallas_call(
        _ada_kernel,
        grid=(DEPTH, n // tn),
        in_specs=[
            pl.BlockSpec((rows, D_MODEL), lambda l, j: (0, 0)),
            pl.BlockSpec((1, D_MODEL, tn), lambda l, j: (l, 0, j)),
            pl.BlockSpec((1, 1, tn), lambda l, j: (l, 0, j)),
        ],
        out_specs=pl.BlockSpec((1, rows, tn), lambda l, j: (l, 0, j)),
        out_shape=jax.ShapeDtypeStruct((DEPTH, rows, n), F32),
        compiler_params=_cparams(("arbitrary", "arbitrary")),
        name="ada_mod",
    )(c_all, w_ada, b_ada.reshape(DEPTH, 1, n))


def _modulated_norm(x, g, scale, shift):
    ms = jnp.mean(x * x, axis=-1, keepdims=True)
    y = x * lax.rsqrt(ms + EPS) * g
    return y * (1.0 + scale) + shift


def _layer_spec(l, shape, **kw):
    zeros = (0,) * len(shape)
    return pl.BlockSpec((None,) + tuple(shape), lambda *_: (l,) + zeros, **kw)


def _inproj_kernel(x_ref, g_ref, sc_ref, sh_ref, wm_ref, wt_ref, om_ref, ot_ref, h_ref):
    j = pl.program_id(1)
    n_main = pl.num_programs(1) - 1

    @pl.when(j == 0)
    def _():
        h_ref[...] = _modulated_norm(x_ref[...], g_ref[...], sc_ref[...], sh_ref[...]).astype(BF16)

    @pl.when(j < n_main)
    def _():
        om_ref[...] = jnp.dot(h_ref[...], wm_ref[...], preferred_element_type=F32)

    @pl.when(j == n_main)
    def _():
        ot_ref[...] = jnp.dot(h_ref[...], wt_ref[...], preferred_element_type=F32)


def _inproj_call(grp, l, x, g, mod, w_main, w_tail):
    m, tm = grp["m"], grp["tm"]
    tn = 1024
    n_main = MAIN_WIDTH // tn
    return pl.pallas_call(
        _inproj_kernel,
        grid=(m // tm, n_main + 1),
        in_specs=[
            pl.BlockSpec((tm, D_MODEL), lambda i, j: (i, 0)),
            _layer_spec(l, (1, D_MODEL)),
            grp["mod_spec"](l, 1, tm),
            grp["mod_spec"](l, 0, tm),
            pl.BlockSpec((None, D_MODEL, tn), lambda i, j: (l, 0, jnp.minimum(j, n_main - 1))),
            _layer_spec(l, (D_MODEL, TAIL_WIDTH)),
        ],
        out_specs=[pl.BlockSpec((tm, tn), lambda i, j: (i, jnp.minimum(j, n_main - 1))),
                   pl.BlockSpec((tm, TAIL_WIDTH), lambda i, j: (i, 0))],
        out_shape=[jax.ShapeDtypeStruct((m, MAIN_WIDTH), F32), jax.ShapeDtypeStruct((m, TAIL_WIDTH), F32)],
        scratch_shapes=[pltpu.VMEM((tm, D_MODEL), BF16)],
        compiler_params=_cparams(("arbitrary", "arbitrary")),
        name="in_proj",
    )(x, g, mod, mod, w_main, w_tail)


def _mlp_kernel(*refs, n_cast):
    h_ref, x_ref, gate_ref, wu_ref, wd_ref = refs[:5]
    src_refs = refs[5:5 + n_cast]
    o_ref = refs[5 + n_cast]
    dst_refs = refs[6 + n_cast:6 + 2 * n_cast]
    acc_ref = refs[6 + 2 * n_cast]
    j = pl.program_id(1)

    @pl.when(j == 0)
    def _():
        acc_ref[...] = jnp.zeros_like(acc_ref)

    u = jnp.dot(h_ref[...], wu_ref[0], preferred_element_type=F32)
    a = jnp.square(jnp.maximum(u, 0.0))
    acc_ref[...] += jnp.dot(a.astype(BF16), wd_ref[0], preferred_element_type=F32)
    for src, dst in zip(src_refs, dst_refs):
        dst[...] = src[...].astype(BF16)

    @pl.when(j == pl.num_programs(1) - 1)
    def _():
        o_ref[...] = x_ref[...] + gate_ref[...] * acc_ref[...]


def _mlp_call(grp, l, h2, x, mod, wu_b, wd_b, next_f32=()):
    m, tm = grp["m"], grp["tm_mlp"]
    tf = 1024
    gi, gj = m // tm, D_FF // tf
    in_specs = [
        pl.BlockSpec((tm, D_MODEL), lambda i, j: (i, 0)),
        pl.BlockSpec((tm, D_MODEL), lambda i, j: (i, 0)),
        grp["mod_spec"](l, 5, tm),
        pl.BlockSpec((1, D_MODEL, tf), lambda i, j: (0, 0, j)),
        pl.BlockSpec((1, tf, D_MODEL), lambda i, j: (0, j, 0)),
    ]
    args = [h2, x, mod, wu_b, wd_b]
    out_specs = [pl.BlockSpec((tm, D_MODEL), lambda i, j: (i, 0))]
    out_shape = [jax.ShapeDtypeStruct((m, D_MODEL), F32)]
    for w in next_f32:
        rows, width = w.shape[1] // (gi * gj), w.shape[2]
        in_specs.append(pl.BlockSpec((1, rows, width), lambda i, j: (l + 1, i * gj + j, 0)))
        out_specs.append(pl.BlockSpec((1, rows, width), lambda i, j: (0, i * gj + j, 0)))
        out_shape.append(jax.ShapeDtypeStruct((1,) + w.shape[1:], BF16))
        args.append(w)
    return pl.pallas_call(
        functools.partial(_mlp_kernel, n_cast=len(next_f32)),
        grid=(gi, gj),
        in_specs=in_specs,
        out_specs=out_specs,
        out_shape=out_shape,
        scratch_shapes=[pltpu.VMEM((tm, D_MODEL), F32)],
        compiler_params=_cparams(("arbitrary", "arbitrary")),
        name="mlp",
    )(*args)


def _row_iota(shape):
    return lax.broadcasted_iota(jnp.int32, shape, 0)


def _lane_iota(shape):
    return lax.broadcasted_iota(jnp.int32, shape, 1)


def _cat(pieces, axis):
    return pieces[0] if len(pieces) == 1 else jnp.concatenate(pieces, axis=axis)


def _cumsum_rows(x, lseq):
    rows = x.shape[0]
    rr = _row_iota((rows, rows))
    cc = _lane_iota((rows, rows))
    keep = cc <= rr
    if lseq < rows:
        keep = jnp.logical_and(keep, (rr // lseq) == (cc // lseq))
    tri = jnp.where(keep, 1.0, 0.0).astype(BF16)
    hi = x.astype(BF16)
    rem = x - hi.astype(F32)
    mid = rem.astype(BF16)
    lo = (rem - mid.astype(F32)).astype(BF16)
    return (jnp.dot(tri, hi, preferred_element_type=F32) + jnp.dot(tri, mid, preferred_element_type=F32)
            + jnp.dot(tri, lo, preferred_element_type=F32))


def _seq_last_rows(x, nseq, lseq):
    return _cat([jnp.broadcast_to(x[(b + 1) * lseq - 1:(b + 1) * lseq, :], (lseq, x.shape[1]))
                 for b in range(nseq)], 0)


def _hgrn_body(pm_ref, par_ref, st_ref, *, nseq, lseq):
    rows = TILE_ROWS
    dh = HGRN_DH
    aq = pm_ref[:, COL_AQ:COL_AQ + W_GROUP]
    xf = pm_ref[:, COL_AF:COL_AF + W_GROUP]
    v = pm_ref[:, COL_AI:COL_AI + W_GROUP]
    q = aq * _sigmoid(aq)
    e = jnp.exp(-jnp.abs(xf))
    inv = 1.0 / (1.0 + e)
    log_sig = jnp.minimum(xf, 0.0) - jnp.log(1.0 + e)
    log_lb = par_ref[0:1, :]
    bterm = par_ref[1:2, :] + log_sig
    lf = jnp.maximum(log_lb, bterm) + jnp.log(1.0 + jnp.exp(-jnp.abs(log_lb - bterm)))
    one_m_lb = par_ref[2:3, :]
    k = one_m_lb * jnp.where(xf >= 0.0, e * inv, inv)
    f = par_ref[4:5, :] + one_m_lb * jnp.where(xf >= 0.0, inv, e * inv)
    cum = _cumsum_rows(lf, lseq)

    row = _row_iota((rows, W_GROUP))
    heads = [slice(hd * dh, (hd + 1) * dh) for hd in range(HGRN_HEADS)]
    o = [jnp.zeros((rows, dh), F32) for _ in heads]

    sub = min(SUBLANES, lseq)
    fm = jnp.where((row & (sub - 1)) == 0, 0.0, f)
    kg = k
    vs = v
    for d in range(sub):
        if d > 0:
            kg = fm * pltpu.roll(kg, 1, axis=0)
            vs = pltpu.roll(vs, 1, axis=0)
        term = q * kg
        for hd, sl in enumerate(heads):
            o[hd] = o[hd] + jnp.sum(term[:, sl], axis=-1, keepdims=True) * vs[:, sl]

    levels = []
    h = lseq // 2
    while h >= sub:
        levels.append(h)
        h //= 2
    if levels:
        rr = _row_iota((rows, rows))
        cc = _lane_iota((rows, rows))
        p = [jnp.zeros((rows, rows), F32) for _ in heads]
        for h in levels:
            upper = (row & h) != 0
            refm = _cat([jnp.broadcast_to(cum[jb * 2 * h + h - 1:jb * 2 * h + h, :], (2 * h, W_GROUP))
                         for jb in range(rows // (2 * h))], 0)
            x = jnp.exp(jnp.where(upper, cum - refm, refm - cum))
            a_side = jnp.where(upper, q * x, 0.0).astype(BF16)
            b_side = jnp.where(upper, 0.0, k * x).astype(BF16)
            if 2 * h < rows:
                same = (rr // (2 * h)) == (cc // (2 * h))
            for hd, sl in enumerate(heads):
                s = lax.dot_general(a_side[:, sl], b_side[:, sl], _NT, preferred_element_type=F32)
                if 2 * h < rows:
                    s = jnp.where(same, s, 0.0)
                p[hd] = p[hd] + s
        for hd, sl in enumerate(heads):
            o[hd] = o[hd] + _dot(p[hd], v[:, sl])

    qe = (q * jnp.exp(cum)).astype(BF16)
    lastm = _seq_last_rows(cum, nseq, lseq)
    kd = k * jnp.exp(lastm - cum)
    dec = jnp.exp(lastm)
    vb = v.astype(BF16)
    seq_of_row = row // lseq
    for hd, sl in enumerate(heads):
        inter = []
        for b in range(nseq):
            rs = slice(b * lseq, (b + 1) * lseq)
            st = st_ref[b, hd]
            inter.append(lax.dot_general(qe[rs, sl], st.astype(BF16), _NT, preferred_element_type=F32))
            kdb = kd if nseq == 1 else jnp.where(seq_of_row == b, kd, 0.0)
            upd = lax.dot_general(vb[:, sl], kdb[:, sl].astype(BF16), _TN, preferred_element_type=F32)
            st_ref[b, hd] = st * dec[b * lseq:b * lseq + 1, sl] + upd
        o[hd] = o[hd] + _cat(inter, 0)

    gate = _sigmoid(pm_ref[:, COL_AG:COL_AG + W_GROUP])
    out = []
    for hd, sl in enumerate(heads):
        ms = jnp.mean(o[hd] * o[hd], axis=-1, keepdims=True)
        out.append(o[hd] * lax.rsqrt(ms + EPS) * par_ref[3:4, sl] * gate[:, sl])
    return out


def _pool_body(pm_ref, pw_ref, sc_ref, hout_ref, ext, i, *, nseq, lseq, pos0):
    stride = lseq + HIST_PAD
    u = pm_ref[:, COL_PU:COL_PU + W_GROUP]
    for b in range(nseq):
        base = b * stride
        ext[base + HIST_PAD - POOL_BUF:base + HIST_PAD, :] = hout_ref[b]
        ext[base + HIST_PAD:base + HIST_PAD + lseq, :] = u[b * lseq:(b + 1) * lseq, :]

    local = _row_iota((TILE_ROWS, POOL_CH)) & (lseq - 1)
    posn = pos0 + i * lseq + local
    out = []
    for gi, win in enumerate(POOL_WINDOWS):
        cs = slice(gi * POOL_CH, (gi + 1) * POOL_CH)
        pieces = []
        for b in range(nseq):
            base = b * stride + HIST_PAD
            s = ext[base:base + lseq, cs]
            for j in range(1, win):
                s = s + ext[pl.ds(base - j, lseq), cs]
            pieces.append(s)
        cnt = jnp.minimum(posn + 1, win).astype(F32)
        pooled = _cat(pieces, 0) / cnt - u[:, cs]
        out.append(_dot(pooled, pw_ref[gi]) * sc_ref[:, cs])

    for b in range(nseq):
        base = b * stride
        hout_ref[b] = ext[base + lseq + HIST_PAD - POOL_BUF:base + lseq + HIST_PAD, :]
    return out


def _expand_heads(z, emat):
    hi = z.astype(BF16)
    lo = (z - hi.astype(F32)).astype(BF16)
    return (jnp.dot(hi, emat, preferred_element_type=F32) + jnp.dot(lo, emat, preferred_element_type=F32))


def _ssd_body(pm_ref, pt_ref, cw_ref, cb_ref, par_ref, par5_ref, cout_ref, st_ref, ext, *, nseq, lseq):
    rows = TILE_ROWS
    hist = SSM_CONV - 1
    stride = lseq + SUBLANES
    gw = SSM_N

    xbc = pm_ref[:, COL_XBC:COL_XBC + SSM_CONV_DIM]
    for b in range(nseq):
        base = b * stride
        ext[base + SUBLANES - hist:base + SUBLANES, :] = cout_ref[b]
        ext[base + SUBLANES:base + SUBLANES + lseq, :] = xbc[b * lseq:(b + 1) * lseq, :]
    acc = None
    for j in range(SSM_CONV):
        sh = _cat([ext[pl.ds(b * stride + SUBLANES - hist + j, lseq), :] for b in range(nseq)], 0)
        t = sh * cw_ref[j:j + 1, :]
        acc = t if acc is None else acc + t
    for b in range(nseq):
        base = b * stride
        cout_ref[b] = ext[base + lseq + SUBLANES - hist:base + lseq + SUBLANES, :]
    acc = acc + cb_ref[...]
    conv = acc * _sigmoid(acc)
    x = conv[:, 0:W_GROUP]
    bm = conv[:, W_GROUP:W_GROUP + SSM_GROUPS * gw]
    cm = conv[:, W_GROUP + SSM_GROUPS * gw:]

    pre = pt_ref[:, COL_DT:COL_DT + LANES] + par_ref[0:1, :]
    dt = jnp.maximum(pre, 0.0) + jnp.log1p(jnp.exp(-jnp.abs(pre)))
    a = dt * (-jnp.exp(par_ref[1:2, :]))
    cum = _cumsum_rows(a, lseq)
    cum_t = cum.T
    dt_t = dt.T

    rr = _row_iota((rows, rows))
    cc = _lane_iota((rows, rows))
    valid = cc <= rr
    if nseq > 1:
        valid = jnp.logical_and(valid, (rr // lseq) == (cc // lseq))

    erow = _row_iota((LANES, W_GROUP))
    ecol = _lane_iota((LANES, W_GROUP))
    emat = jnp.where(ecol // SSM_P == erow, 1.0, 0.0).astype(BF16)

    lane = _lane_iota((rows, LANES))
    lo_half = lane < SSM_P
    heads_per_group = SSM_HEADS // SSM_GROUPS
    y_chunks = []
    for g in range(SSM_GROUPS):
        gs = slice(g * gw, (g + 1) * gw)
        cb = lax.dot_general(cm[:, gs].astype(BF16), bm[:, gs].astype(BF16), _NT, preferred_element_type=F32)
        for jc in range(heads_per_group // 2):
            chunk = g * (heads_per_group // 2) + jc
            xc = x[:, chunk * LANES:(chunk + 1) * LANES]
            yc = None
            for half in range(2):
                r = 2 * chunk + half
                seg = jnp.broadcast_to(cum[:, r:r + 1], (rows, rows)) - cum_t[r:r + 1, :]
                wts = cb * jnp.exp(jnp.where(valid, seg, NEG_BIG)) * dt_t[r:r + 1, :]
                xm = jnp.where(lo_half, xc, 0.0) if half == 0 else jnp.where(lo_half, 0.0, xc)
                t = _dot(wts, xm)
                yc = t if yc is None else yc + t
            y_chunks.append(yc)
    y = jnp.concatenate(y_chunks, axis=1)

    lastm = _seq_last_rows(cum, nseq, lseq)
    e_cum = _expand_heads(jnp.exp(cum), emat)
    e_wst = _expand_heads(dt * jnp.exp(lastm - cum), emat)
    e_dec = _expand_heads(jnp.exp(lastm), emat)
    xs = x * e_wst
    row5 = _row_iota((rows, W_GROUP))
    inter_groups = []
    pw = heads_per_group * SSM_P
    for g in range(SSM_GROUPS):
        gs = slice(g * gw, (g + 1) * gw)
        ps = slice(g * pw, (g + 1) * pw)
        inter = []
        for b in range(nseq):
            rs = slice(b * lseq, (b + 1) * lseq)
            st = st_ref[b, g]
            inter.append(_dot(cm[rs, gs], st))
            xsb = xs if nseq == 1 else jnp.where(row5 // lseq == b, xs, 0.0)
            upd = _dot_tn(bm[:, gs], xsb[:, ps])
            st_ref[b, g] = st * e_dec[b * lseq:b * lseq + 1, ps] + upd
        inter_groups.append(_cat(inter, 0))
    y = y + jnp.concatenate(inter_groups, axis=1) * e_cum
    y = y + x * par5_ref[0:1, :]
    z = pm_ref[:, COL_CZ:COL_CZ + W_GROUP]
    y = y * (z * _sigmoid(z))
    ms = jnp.mean(y * y, axis=-1, keepdims=True)
    return y * lax.rsqrt(ms + EPS) * par5_ref[1:2, :]


def _head_rmsnorm(x, g, lo_half):
    sq = x * x
    s_lo = jnp.sum(jnp.where(lo_half, sq, 0.0), axis=-1, keepdims=True)
    s_hi = jnp.sum(jnp.where(lo_half, 0.0, sq), axis=-1, keepdims=True)
    ms = jnp.where(lo_half, s_lo, s_hi) * (1.0 / ATTN_DH)
    return x * lax.rsqrt(ms + EPS) * g


def _rope(x, cos, sin_signed, upper_half):
    partner = jnp.where(upper_half, pltpu.roll(x, ATTN_DH // 2, axis=1), pltpu.roll(x, LANES - ATTN_DH // 2, axis=1))
    return x * cos + partner * sin_signed


def _dup_head(x, g, lane):
    own = jnp.where((lane // ATTN_DH) == g, x, 0.0)
    return own + pltpu.roll(own, ATTN_DH, axis=1)


def _swa_body(pt_ref, cos_ref, sin_ref, par_ref, sink_ref, kout_ref, vout_ref, i, *, nseq, lseq, has_state):
    rows = TILE_ROWS
    rep = ATTN_HEADS // ATTN_KV
    lane = _lane_iota((rows, LANES))
    lo_half = lane < ATTN_DH
    upper_half = (lane & (ATTN_DH // 2)) != 0
    cos = cos_ref[...]
    sin = sin_ref[...]
    kn = _rope(_head_rmsnorm(pt_ref[:, COL_DK:COL_DK + LANES], par_ref[1:2, :], lo_half), cos, sin, upper_half)
    vn = pt_ref[:, COL_DV:COL_DV + LANES]
    scale = ATTN_DH ** -0.5
    qs = []
    for c in range(ATTN_HEADS // 2):
        qc = pt_ref[:, COL_DQ + c * LANES:COL_DQ + (c + 1) * LANES]
        qs.append(_rope(_head_rmsnorm(qc, par_ref[0:1, :], lo_half), cos, sin, upper_half) * scale)

    mq = rep * lseq
    qrow = _row_iota((mq, LANES))
    q_local = qrow & (lseq - 1)
    kcol = _lane_iota((mq, LANES))
    hist_ok = jnp.logical_or(has_state, i > 0)
    hist_valid = jnp.logical_and(kcol > q_local, hist_ok)
    lane_w = _lane_iota((WINDOW, LANES))

    out_rows = [[None] * nseq for _ in range(ATTN_HEADS // 2)]
    for g in range(ATTN_KV):
        k_new = _dup_head(kn, g, lane).astype(BF16)
        v_new = _dup_head(vn, g, lane).astype(BF16)
        for b in range(nseq):
            rs = slice(b * lseq, (b + 1) * lseq)
            q4 = []
            sink_rows = []
            for r in range(rep):
                hidx = g * rep + r
                qc = qs[hidx // 2][rs, :]
                lo_l = lo_half[0:lseq, :]
                q4.append(jnp.where(lo_l, qc, 0.0) if hidx % 2 == 0 else jnp.where(lo_l, 0.0, qc))
                sink_rows.append(jnp.broadcast_to(sink_ref[hidx:hidx + 1, 0:1], (lseq, 1)))
            q4 = jnp.concatenate(q4, axis=0).astype(BF16)
            sink = jnp.concatenate(sink_rows, axis=0)
            k_hist = _dup_head(kout_ref[b], g, lane_w).astype(BF16)
            v_hist = _dup_head(vout_ref[b], g, lane_w).astype(BF16)
            s_h = lax.dot_general(q4, k_hist, _NT, preferred_element_type=F32)
            s_n = lax.dot_general(q4, k_new, _NT, preferred_element_type=F32)
            s_h = jnp.where(hist_valid, s_h, NEG_BIG)
            new_valid = jnp.logical_and(kcol // lseq == b, (kcol & (lseq - 1)) <= q_local)
            s_n = jnp.where(new_valid, s_n, NEG_BIG)
            m = jnp.maximum(jnp.maximum(jnp.max(s_h, axis=-1, keepdims=True),
                                        jnp.max(s_n, axis=-1, keepdims=True)), sink)
            p_h = jnp.exp(s_h - m)
            p_n = jnp.exp(s_n - m)
            den = (jnp.sum(p_h, axis=-1, keepdims=True) + jnp.sum(p_n, axis=-1, keepdims=True)
                   + jnp.exp(sink - m))
            o4 = (jnp.dot(p_h.astype(BF16), v_hist, preferred_element_type=F32)
                  + jnp.dot(p_n.astype(BF16), v_new, preferred_element_type=F32)) / den
            for pair in range(rep // 2):
                c = (g * rep) // 2 + pair
                o_lo = o4[(2 * pair) * lseq:(2 * pair + 1) * lseq, :]
                o_hi = o4[(2 * pair + 1) * lseq:(2 * pair + 2) * lseq, :]
                out_rows[c][b] = jnp.where(lo_half[0:lseq, :], o_lo, o_hi)

    for b in range(nseq):
        rs = slice(b * lseq, (b + 1) * lseq)
        if lseq < WINDOW:
            keep_k = kout_ref[b, lseq:WINDOW, :]
            keep_v = vout_ref[b, lseq:WINDOW, :]
            kout_ref[b, 0:WINDOW - lseq, :] = keep_k
            vout_ref[b, 0:WINDOW - lseq, :] = keep_v
        kout_ref[b, WINDOW - lseq:WINDOW, :] = kn[rs, :]
        vout_ref[b, WINDOW - lseq:WINDOW, :] = vn[rs, :]
    return [_cat(out_rows[c], 0) for c in range(ATTN_HEADS // 2)]


N_STATES = 6


def _transpose_heads(st_ref, n):
    for b in range(n):
        for hd in range(HGRN_HEADS):
            st_ref[b, hd] = st_ref[b, hd].T


def _mix_kernel(*refs, nseq, lseq, has_state, pos0):
    (pm_ref, pt_ref, x_ref, gate_ref, g2_ref, sc2_ref, sh2_ref, w_ref, cos_ref, sin_ref, hpar_ref, pw_ref, psc_ref,
     cw_ref, cb_ref, spar_ref, spar5_ref, apar_ref, sink_ref) = refs[:19]
    rest = refs[19:]
    if has_state:
        init_refs, rest = rest[:N_STATES], rest[N_STATES:]
    else:
        init_refs = (None,) * N_STATES
    o_ref, h2_ref = rest[:2]
    state_refs = rest[2:2 + N_STATES]
    pool_ext, conv_ext = rest[2 + N_STATES:]
    nh_ref, np_ref, nc_ref, ns_ref, nk_ref, nv_ref = state_refs
    i = pl.program_id(1)

    @pl.when(i == 0)
    def _():
        for dst, src in zip(state_refs, init_refs):
            if src is None:
                dst[...] = jnp.zeros_like(dst)
            elif dst is nh_ref:
                for b in range(nseq):
                    for hd in range(HGRN_HEADS):
                        dst[b, hd] = src[b, hd].T
            else:
                dst[...] = src[...]

    ya = _hgrn_body(pm_ref, hpar_ref, nh_ref, nseq=nseq, lseq=lseq)
    yb = _pool_body(pm_ref, pw_ref, psc_ref, np_ref, pool_ext, i, nseq=nseq, lseq=lseq, pos0=pos0)
    yc = _ssd_body(pm_ref, pt_ref, cw_ref, cb_ref, spar_ref, spar5_ref, nc_ref, ns_ref, conv_ext, nseq=nseq, lseq=lseq)
    yd = _swa_body(pt_ref, cos_ref, sin_ref, apar_ref, sink_ref, nk_ref, nv_ref, i, nseq=nseq, lseq=lseq,
                   has_state=has_state)
    y = jnp.concatenate([t.astype(BF16) for t in ya + yb + [yc] + yd], axis=1)
    mixed = jnp.dot(y, w_ref[...], preferred_element_type=F32)
    x1 = x_ref[...] + gate_ref[...] * mixed
    o_ref[...] = x1
    h2_ref[...] = _modulated_norm(x1, g2_ref[...], sc2_ref[...], sh2_ref[...]).astype(BF16)

    @pl.when(i == pl.num_programs(1) - 1)
    def _():
        _transpose_heads(nh_ref, nseq)


def _mix_call(grp, l, x, proj, tail, g2, mod, w_out, cos, sin, pars, states):
    nseq, lseq, no, nt = grp["nseq"], grp["lseq"], grp["no"], grp["nt"]
    has_state = states is not None
    hpar, pool_w, pool_sc, conv_w, conv_b, spar, spar5, apar, sink = pars
    tails = [(HGRN_HEADS, HGRN_DH, HGRN_DH), (POOL_BUF, W_GROUP), (SSM_CONV - 1, SSM_CONV_DIM),
             (SSM_GROUPS, SSM_N, (SSM_HEADS // SSM_GROUPS) * SSM_P),
             (WINDOW, ATTN_KV * ATTN_DH), (WINDOW, ATTN_KV * ATTN_DH)]

    def row_spec(width):
        return pl.BlockSpec((TILE_ROWS, width), lambda o, i: (o * nt + i, 0))

    in_specs = [
        row_spec(MAIN_WIDTH), row_spec(TAIL_WIDTH), row_spec(D_MODEL),
        grp["mod_spec"](l, 2, TILE_ROWS, lambda o, i: o * nt + i),
        _layer_spec(l, (1, D_MODEL)),
        grp["mod_spec"](l, 4, TILE_ROWS, lambda o, i: o * nt + i),
        grp["mod_spec"](l, 3, TILE_ROWS, lambda o, i: o * nt + i),
        _layer_spec(l, (D_MODEL, D_MODEL), pipeline_mode=pl.Buffered(1)),
        pl.BlockSpec((TILE_ROWS, LANES), lambda o, i: (i, 0)),
        pl.BlockSpec((TILE_ROWS, LANES), lambda o, i: (i, 0)),
        _layer_spec(l, (8, W_GROUP)), _layer_spec(l, (len(POOL_WINDOWS), POOL_CH, POOL_CH)),
        _layer_spec(l, (1, W_GROUP)), _layer_spec(l, (SSM_CONV, SSM_CONV_DIM)), _layer_spec(l, (1, SSM_CONV_DIM)),
        _layer_spec(l, (8, LANES)), _layer_spec(l, (8, W_GROUP)), _layer_spec(l, (8, LANES)), _layer_spec(l, (8, LANES)),
    ]
    args = [proj, tail, x, mod, g2, mod, mod, w_out, cos, sin, hpar, pool_w, pool_sc, conv_w, conv_b, spar, spar5,
            apar, sink]
    if has_state:
        for t in tails:
            zeros = (0,) * len(t)
            in_specs.append(pl.BlockSpec((None, nseq) + t, lambda o, i, zeros=zeros: (l, o) + zeros,
                                         pipeline_mode=pl.Buffered(1)))
        args += list(states)
    out_specs = [row_spec(D_MODEL), row_spec(D_MODEL)]
    out_shape = [jax.ShapeDtypeStruct((grp["m"], D_MODEL), F32), jax.ShapeDtypeStruct((grp["m"], D_MODEL), BF16)]
    state_mode = dict(pipeline_mode=pl.Buffered(1)) if nseq > 1 else {}
    for t in tails:
        zeros = (0,) * len(t)
        out_specs.append(pl.BlockSpec((nseq,) + t, lambda o, i, zeros=zeros: (o,) + zeros, **state_mode))
        out_shape.append(jax.ShapeDtypeStruct((grp["nb"],) + t, F32))
    return pl.pallas_call(
        functools.partial(_mix_kernel, nseq=nseq, lseq=lseq, has_state=has_state, pos0=grp["pos0"]),
        grid=(no, nt),
        in_specs=in_specs,
        out_specs=out_specs,
        out_shape=out_shape,
        scratch_shapes=[pltpu.VMEM((nseq * (lseq + HIST_PAD), W_GROUP), F32),
                        pltpu.VMEM((nseq * (lseq + SUBLANES), SSM_CONV_DIM), F32)],
        compiler_params=_cparams(("arbitrary", "arbitrary")),
        name="mix",
    )(*args)


def _front_kernel(*refs, nsub, pos0):
    (x_ref, g_ref, g2_ref, mod_ref, wm_ref, wt_ref, w_ref, cos_ref, sin_ref, hpar_ref, pw_ref, psc_ref, cw_ref,
     cb_ref, spar_ref, spar5_ref, apar_ref, sink_ref) = refs[:18]
    o_ref, h2_ref = refs[18:20]
    state_refs = refs[20:20 + N_STATES]
    pm_scr, pt_scr = refs[20 + N_STATES:22 + N_STATES]
    ext_refs = refs[22 + N_STATES:]
    rows = TILE_ROWS
    i = pl.program_id(0)

    @pl.when(i == 0)
    def _():
        for dst in state_refs:
            dst[...] = jnp.zeros_like(dst)

    h = _cat([_modulated_norm(x_ref[s], g_ref[...], mod_ref[s * N_MOD + 1], mod_ref[s * N_MOD + 0]).astype(BF16)
              for s in range(nsub)], 0)
    pm_scr[...] = jnp.dot(h, wm_ref[...], preferred_element_type=F32)
    pt_scr[...] = jnp.dot(h, wt_ref[...], preferred_element_type=F32)

    ys = []
    for s in range(nsub):
        pm = pm_scr.at[pl.ds(s * rows, rows)]
        pt = pt_scr.at[pl.ds(s * rows, rows)]
        nh, npool, nc, ns, nk, nv = (r.at[pl.ds(s, 1)] for r in state_refs)
        ya = _hgrn_body(pm, hpar_ref, nh, nseq=1, lseq=rows)
        yb = _pool_body(pm, pw_ref, psc_ref, npool, ext_refs[2 * s], i, nseq=1, lseq=rows, pos0=pos0)
        yc = _ssd_body(pm, pt, cw_ref, cb_ref, spar_ref, spar5_ref, nc, ns, ext_refs[2 * s + 1], nseq=1, lseq=rows)
        yd = _swa_body(pt, cos_ref, sin_ref, apar_ref, sink_ref, nk, nv, i, nseq=1, lseq=rows, has_state=False)
        ys.append(jnp.concatenate([t.astype(BF16) for t in ya + yb + [yc] + yd], axis=1))
    mixed = jnp.dot(_cat(ys, 0), w_ref[...], preferred_element_type=F32)
    for s in range(nsub):
        x1 = x_ref[s] + mod_ref[s * N_MOD + 2] * mixed[s * rows:(s + 1) * rows, :]
        o_ref[s] = x1
        h2_ref[s] = _modulated_norm(x1, g2_ref[...], mod_ref[s * N_MOD + 4], mod_ref[s * N_MOD + 3]).astype(BF16)

    @pl.when(i == pl.num_programs(0) - 1)
    def _():
        _transpose_heads(state_refs[0], nsub)


def _front_call(grp, l, x, g, g2, mod, w_in_b, w_tail, w_out, cos, sin, pars):
    nb, nt = grp["nb"], grp["nt"]
    hpar, pool_w, pool_sc, conv_w, conv_b, spar, spar5, apar, sink = pars
    tails = [(HGRN_HEADS, HGRN_DH, HGRN_DH), (POOL_BUF, W_GROUP), (SSM_CONV - 1, SSM_CONV_DIM),
             (SSM_GROUPS, SSM_N, (SSM_HEADS // SSM_GROUPS) * SSM_P),
             (WINDOW, ATTN_KV * ATTN_DH), (WINDOW, ATTN_KV * ATTN_DH)]
    once = dict(pipeline_mode=pl.Buffered(1))
    x_spec = pl.BlockSpec((nb, TILE_ROWS, D_MODEL), lambda i: (0, i, 0))
    in_specs = [
        x_spec, _layer_spec(l, (1, D_MODEL)), _layer_spec(l, (1, D_MODEL)), _layer_spec(l, (nb * N_MOD, 1, D_MODEL)),
        _layer_spec(l, (D_MODEL, MAIN_WIDTH), **once), _layer_spec(l, (D_MODEL, TAIL_WIDTH), **once),
        _layer_spec(l, (D_MODEL, D_MODEL), **once),
        pl.BlockSpec((TILE_ROWS, LANES), lambda i: (i, 0)), pl.BlockSpec((TILE_ROWS, LANES), lambda i: (i, 0)),
        _layer_spec(l, (8, W_GROUP)), _layer_spec(l, (len(POOL_WINDOWS), POOL_CH, POOL_CH)),
        _layer_spec(l, (1, W_GROUP)), _layer_spec(l, (SSM_CONV, SSM_CONV_DIM)), _layer_spec(l, (1, SSM_CONV_DIM)),
        _layer_spec(l, (8, LANES)), _layer_spec(l, (8, W_GROUP)), _layer_spec(l, (8, LANES)), _layer_spec(l, (8, LANES)),
    ]
    args = [x, g, g2, mod, w_in_b, w_tail, w_out, cos, sin, hpar, pool_w, pool_sc, conv_w, conv_b, spar, spar5, apar, sink]
    out_specs = [x_spec, x_spec]
    out_shape = [jax.ShapeDtypeStruct(x.shape, F32), jax.ShapeDtypeStruct(x.shape, BF16)]
    for t in tails:
        zeros = (0,) * (len(t) + 1)
        out_specs.append(pl.BlockSpec((nb,) + t, lambda i, zeros=zeros: zeros))
        out_shape.append(jax.ShapeDtypeStruct((nb,) + t, F32))
    scratch = [pltpu.VMEM((nb * TILE_ROWS, MAIN_WIDTH), F32), pltpu.VMEM((nb * TILE_ROWS, TAIL_WIDTH), F32)]
    for _ in range(nb):
        scratch += [pltpu.VMEM((TILE_ROWS + HIST_PAD, W_GROUP), F32),
                    pltpu.VMEM((TILE_ROWS + SUBLANES, SSM_CONV_DIM), F32)]
    return pl.pallas_call(
        functools.partial(_front_kernel, nsub=nb, pos0=grp["pos0"]),
        grid=(nt,),
        in_specs=in_specs,
        out_specs=out_specs,
        out_shape=out_shape,
        scratch_shapes=scratch,
        compiler_params=_cparams(("arbitrary",)),
        name="front",
    )(*args)


def _rope_tables(pos):
    half = ATTN_DH // 2
    inv = ROPE_THETA ** (-jnp.arange(half, dtype=F32) / half)
    ang = pos.astype(F32)[:, None] * inv[None]
    cos = jnp.tile(jnp.cos(ang), (1, LANES // half))
    sin = jnp.sin(ang)
    sin_signed = jnp.tile(jnp.concatenate([-sin, sin], axis=1), (1, LANES // ATTN_DH))
    return cos, sin_signed


def _pad_lanes(v, width):
    return jnp.pad(v, (0, width - v.shape[0]))


def _rows8(rows_list, width):
    out = jnp.zeros((8, width), F32)
    for r, v in enumerate(rows_list):
        out = out.at[r].set(v)
    return out


def _make_group(nb, seq_len, pos0, mod_rows, tm):
    m = nb * seq_len
    if seq_len >= TILE_ROWS:
        nseq, lseq = 1, TILE_ROWS
        no, nt = nb, seq_len // TILE_ROWS
    else:
        nseq, lseq = TILE_ROWS // seq_len, seq_len
        no, nt = m // TILE_ROWS, 1
    per_row = mod_rows.shape[-2] != 1

    def mod_spec(l, kind, tile, row_tile=lambda i, *_: i):
        if per_row:
            def imap(*idx):
                return (l, kind, row_tile(*idx), 0)
            return pl.BlockSpec((None, None, tile, D_MODEL), imap)

        def imap(*idx):
            return (l, (row_tile(*idx) * tile) // seq_len * N_MOD + kind, 0, 0)
        return pl.BlockSpec((None, None, 1, D_MODEL), imap)

    return dict(nb=nb, seq=seq_len, m=m, tm=tm, tm_mlp=min(m, MLP_ROWS), nseq=nseq, lseq=lseq, no=no, nt=nt,
                pos0=pos0, mod_spec=mod_spec)


def _trunk(grp, x, mod, states, cos, sin, wts, bf16_w):
    w_in_b, w_tail, w_out_b, f32_w, norm1_g, norm2_g, mix_pars = wts
    outs = [[] for _ in range(N_STATES)]
    for l in range(DEPTH):
        wu_b, wd_b = bf16_w[l]
        if grp["nseq"] == 1 and states is None:
            x3 = x.reshape(grp["nb"], grp["seq"], D_MODEL)
            x3, h2, *new_states = _front_call(grp, l, x3, norm1_g, norm2_g, mod, w_in_b, w_tail, w_out_b, cos, sin,
                                              mix_pars)
            x, h2 = x3.reshape(grp["m"], D_MODEL), h2.reshape(grp["m"], D_MODEL)
        else:
            proj, tail = _inproj_call(grp, l, x, norm1_g, mod, w_in_b, w_tail)
            x, h2, *new_states = _mix_call(grp, l, x, proj, tail, norm2_g, mod, w_out_b, cos, sin, mix_pars, states)
        if l + 1 < DEPTH and bf16_w[l + 1] is None:
            x, *converted = _mlp_call(grp, l, h2, x, mod, wu_b, wd_b, f32_w)
            bf16_w[l + 1] = converted
        else:
            x, = _mlp_call(grp, l, h2, x, mod, wu_b, wd_b)
        for lst, val in zip(outs, new_states):
            lst.append(val)
    return x, [jnp.stack(o) for o in outs]


def _ssm_state_to_kernel(s):
    lead = s.shape[:-3]
    r = SSM_HEADS // SSM_GROUPS
    s = s.reshape(lead + (SSM_GROUPS, r, SSM_P, SSM_N))
    s = jnp.moveaxis(s, -1, -3)
    return s.reshape(lead + (SSM_GROUPS, SSM_N, r * SSM_P))


def _ssm_state_from_kernel(s):
    lead = s.shape[:-3]
    r = SSM_HEADS // SSM_GROUPS
    s = s.reshape(lead + (SSM_GROUPS, SSM_N, r, SSM_P))
    s = jnp.moveaxis(s, -3, -1)
    return s.reshape(lead + (SSM_HEADS, SSM_P, SSM_N))


def kernel(x_prompt, x_sample, c_prompt, c_sample, state_hgrn, state_pool, state_ssm, state_conv, cache_k, cache_v, norm1_g, norm2_g, w_ada, b_ada, w_in, hgrn_lb_logits, hgrn_norm_g, pool_w, pool_scale, conv_w, conv_b, dt_bias, a_log, d_skip, ssm_norm_g, q_norm_g, k_norm_g, sinks, w_out, w_up, w_down):
    bp, seq, _ = x_prompt.shape
    bs, dseq, _ = x_sample.shape

    w_in_b = w_in.astype(BF16)
    dt_end = MAIN_WIDTH + SSM_HEADS
    w_tail = jnp.concatenate(
        [w_in_b[:, :, dt_end:], w_in_b[:, :, MAIN_WIDTH:dt_end],
         jnp.zeros((DEPTH, D_MODEL, LANES - SSM_HEADS), BF16)], axis=-1)
    w_out_b = w_out.astype(BF16)
    f32_w = (w_up, w_down)
    bf16_w = [[w[0:1].astype(BF16) for w in f32_w]] + [None] * (DEPTH - 1)
    pool_w_b = pool_w.astype(BF16)

    p = jax.nn.softmax(hgrn_lb_logits.astype(F32), axis=0)
    cs = jnp.cumsum(p, axis=0)
    lbs = cs - cs[:1]
    hgrn_par = jnp.stack([_rows8([jnp.log(lbs[l]), jnp.log1p(-lbs[l]), 1.0 - lbs[l], hgrn_norm_g[l], lbs[l]], W_GROUP)
                          for l in range(DEPTH)])
    ssd_par = jnp.stack([_rows8([_pad_lanes(dt_bias[l], LANES), _pad_lanes(a_log[l], LANES)], LANES)
                         for l in range(DEPTH)])
    ssd_par5 = jnp.stack([_rows8([jnp.repeat(d_skip[l], SSM_P), ssm_norm_g[l]], W_GROUP) for l in range(DEPTH)])
    swa_par = jnp.stack([_rows8([jnp.tile(q_norm_g[l], 2), jnp.tile(k_norm_g[l], 2)], LANES) for l in range(DEPTH)])
    sink_par = jnp.broadcast_to(sinks[:, :, None], (DEPTH, ATTN_HEADS, LANES))
    mix_pars = (hgrn_par, pool_w_b, pool_scale.reshape(DEPTH, 1, W_GROUP), conv_w,
                conv_b.reshape(DEPTH, 1, SSM_CONV_DIM), ssd_par, ssd_par5, swa_par, sink_par)
    wts = (w_in_b, w_tail, w_out_b, f32_w, norm1_g.reshape(DEPTH, 1, D_MODEL),
           norm2_g.reshape(DEPTH, 1, D_MODEL), mix_pars)

    c_all = jnp.concatenate([c_prompt, c_sample], axis=0)
    mod_all = _ada_call(c_all, w_ada, b_ada)
    mod_p = mod_all[:, :bp].reshape(DEPTH, bp * N_MOD, 1, D_MODEL)
    mod_s = mod_all[:, bp:].reshape(DEPTH, bs, N_MOD, D_MODEL)
    mod_s = jnp.repeat(jnp.moveaxis(mod_s, 2, 1), dseq, axis=2)

    grp_p = _make_group(bp, seq, 0, mod_p, 1024)
    grp_s = _make_group(bs, dseq, PAST_LEN, mod_s, bs * dseq)

    cos_p, sin_p = _rope_tables(jnp.arange(seq))
    cos_s, sin_s = _rope_tables(PAST_LEN + (jnp.arange(TILE_ROWS) % dseq))

    y_p, st_p = _trunk(grp_p, x_prompt.reshape(bp * seq, D_MODEL), mod_p, None, cos_p, sin_p, wts, bf16_w)

    kv_flat = (DEPTH, bs, WINDOW, ATTN_KV * ATTN_DH)
    states = (state_hgrn, state_pool, state_conv, _ssm_state_to_kernel(state_ssm),
              cache_k.reshape(kv_flat), cache_v.reshape(kv_flat))
    y_s, st_s = _trunk(grp_s, x_sample.reshape(bs * dseq, D_MODEL), mod_s, states, cos_s, sin_s, wts, bf16_w)

    def finish(st, nb):
        n_h, n_p, n_c, n_s, n_k, n_v = st
        return (n_h, n_p, _ssm_state_from_kernel(n_s), n_c,
                n_k.reshape(DEPTH, nb, WINDOW, ATTN_KV, ATTN_DH), n_v.reshape(DEPTH, nb, WINDOW, ATTN_KV, ATTN_DH))

    return ((y_p.reshape(bp, seq, D_MODEL), y_s.reshape(bs, dseq, D_MODEL)) + finish(st_p, bp) + finish(st_s, bs))
```

```python
import functools

import jax
import jax.numpy as jnp
from jax import lax
from jax.experimental import pallas as pl
from jax.experimental.pallas import tpu as pltpu

F32 = jnp.float32
BF16 = jnp.bfloat16

D_MODEL = 2048
DEPTH = 4
PAST_LEN = 16384
W_GROUP = 512
HGRN_HEADS = 4
HGRN_DH = 128
POOL_WINDOWS = (2, 4, 8, 16)
POOL_CH = 128
POOL_BUF = 15
SSM_HEADS = 8
SSM_P = 64
SSM_N = 128
SSM_GROUPS = 2
SSM_CONV = 4
SSM_CONV_DIM = 1024
ATTN_HEADS = 8
ATTN_KV = 2
ATTN_DH = 64
WINDOW = 128
ROPE_THETA = 10000.0
D_FF = 4 * D_MODEL
N_MOD = 6
EPS = 1e-6

LANES = 128
SUBLANES = 8
TILE_ROWS = 128
MLP_ROWS = 512
HIST_PAD = 16
VMEM_LIMIT = 56 * 1024 * 1024

COL_AQ, COL_AF, COL_AI, COL_AG = 0, 512, 1024, 1536
COL_PU, COL_CZ, COL_XBC = 2048, 2560, 3072
MAIN_WIDTH = 4096
COL_DQ, COL_DK, COL_DV, COL_DT = 0, 512, 640, 768
TAIL_WIDTH = 896
NEG_BIG = -1e30

_NT = (((1,), (1,)), ((), ()))
_TN = (((0,), (0,)), ((), ()))


def _dot(a, b):
    return jnp.dot(a.astype(BF16), b.astype(BF16), preferred_element_type=F32)


def _dot_tn(a, b):
    return lax.dot_general(a.astype(BF16), b.astype(BF16), _TN, preferred_element_type=F32)


def _sigmoid(x):
    return 1.0 / (1.0 + jnp.exp(-x))


def _cparams(sem):
    return pltpu.CompilerParams(dimension_semantics=sem, vmem_limit_bytes=VMEM_LIMIT)


def _ada_kernel(c_ref, w_ref, b_ref, wi_ref, wo_ref, o_ref, wib_ref, wob_ref):
    c = c_ref[...]
    s = c * _sigmoid(c)
    o_ref[0] = _dot(s, w_ref[0]) + b_ref[0]
    wib_ref[...] = wi_ref[...].astype(BF16)
    wob_ref[...] = wo_ref[...].astype(BF16)


def _ada_call(c_all, w_ada, b_ada, w_in, w_out):
    rows = c_all.shape[0]
    n = w_ada.shape[-1]
    tn = 768
    steps = n // tn
    slab = D_MODEL // steps
    in_width = w_in.shape[-1]
    return pl.pallas_call(
        _ada_kernel,
        grid=(DEPTH, steps),
        in_specs=[
            pl.BlockSpec((rows, D_MODEL), lambda l, j: (0, 0)),
            pl.BlockSpec((1, D_MODEL, tn), lambda l, j: (l, 0, j)),
            pl.BlockSpec((1, 1, tn), lambda l, j: (l, 0, j)),
            pl.BlockSpec((1, slab, in_width), lambda l, j: (l, j, 0)),
            pl.BlockSpec((1, slab, D_MODEL), lambda l, j: (l, j, 0)),
        ],
        out_specs=[pl.BlockSpec((1, rows, tn), lambda l, j: (l, 0, j)),
                   pl.BlockSpec((1, slab, in_width), lambda l, j: (l, j, 0)),
                   pl.BlockSpec((1, slab, D_MODEL), lambda l, j: (l, j, 0))],
        out_shape=[jax.ShapeDtypeStruct((DEPTH, rows, n), F32),
                   jax.ShapeDtypeStruct(w_in.shape, BF16), jax.ShapeDtypeStruct(w_out.shape, BF16)],
        compiler_params=_cparams(("arbitrary", "arbitrary")),
        name="ada_mod",
    )(c_all, w_ada, b_ada.reshape(DEPTH, 1, n), w_in, w_out)


def _modulated_norm(x, g, scale, shift):
    ms = jnp.mean(x * x, axis=-1, keepdims=True)
    y = x * lax.rsqrt(ms + EPS) * g
    return y * (1.0 + scale) + shift


def _layer_spec(l, shape, **kw):
    zeros = (0,) * len(shape)
    return pl.BlockSpec((None,) + tuple(shape), lambda *_: (l,) + zeros, **kw)


def _inproj_kernel(x_ref, g_ref, sc_ref, sh_ref, wm_ref, wt_ref, om_ref, ot_ref, h_ref):
    j = pl.program_id(1)
    n_main = pl.num_programs(1) - 1

    @pl.when(j == 0)
    def _():
        h_ref[...] = _modulated_norm(x_ref[...], g_ref[...], sc_ref[...], sh_ref[...]).astype(BF16)

    @pl.when(j < n_main)
    def _():
        om_ref[...] = jnp.dot(h_ref[...], wm_ref[...], preferred_element_type=F32)

    @pl.when(j == n_main)
    def _():
        ot_ref[...] = jnp.dot(h_ref[...], wt_ref[...], preferred_element_type=F32)


def _inproj_call(grp, l, x, g, mod, w_main, w_tail):
    m, tm = grp["m"], grp["tm"]
    tn = 1024
    n_main = MAIN_WIDTH // tn
    return pl.pallas_call(
        _inproj_kernel,
        grid=(m // tm, n_main + 1),
        in_specs=[
            pl.BlockSpec((tm, D_MODEL), lambda i, j: (i, 0)),
            _layer_spec(l, (1, D_MODEL)),
            grp["mod_spec"](l, 1, tm),
            grp["mod_spec"](l, 0, tm),
            pl.BlockSpec((None, D_MODEL, tn), lambda i, j: (l, 0, jnp.minimum(j, n_main - 1))),
            _layer_spec(l, (D_MODEL, TAIL_WIDTH)),
        ],
        out_specs=[pl.BlockSpec((tm, tn), lambda i, j: (i, jnp.minimum(j, n_main - 1))),
                   pl.BlockSpec((tm, TAIL_WIDTH), lambda i, j: (i, 0))],
        out_shape=[jax.ShapeDtypeStruct((m, MAIN_WIDTH), F32), jax.ShapeDtypeStruct((m, TAIL_WIDTH), F32)],
        scratch_shapes=[pltpu.VMEM((tm, D_MODEL), BF16)],
        compiler_params=_cparams(("arbitrary", "arbitrary")),
        name="in_proj",
    )(x, g, mod, mod, w_main, w_tail)


def _mlp_kernel(*refs, n_cast):
    h_ref, x_ref, gate_ref, wu_ref, wd_ref = refs[:5]
    src_refs = refs[5:5 + n_cast]
    o_ref = refs[5 + n_cast]
    dst_refs = refs[6 + n_cast:6 + 2 * n_cast]
    acc_ref = refs[6 + 2 * n_cast]
    j = pl.program_id(1)

    @pl.when(j == 0)
    def _():
        acc_ref[...] = jnp.zeros_like(acc_ref)

    u = jnp.dot(h_ref[...], wu_ref[0], preferred_element_type=F32)
    a = jnp.square(jnp.maximum(u, 0.0))
    acc_ref[...] += jnp.dot(a.astype(BF16), wd_ref[0], preferred_element_type=F32)
    for src, dst in zip(src_refs, dst_refs):
        dst[...] = src[...].astype(BF16)

    @pl.when(j == pl.num_programs(1) - 1)
    def _():
        o_ref[...] = x_ref[...] + gate_ref[...] * acc_ref[...]


def _mlp_call(grp, l, h2, x, mod, wu_b, wd_b, next_f32=()):
    m, tm = grp["m"], grp["tm_mlp"]
    tf = 1024
    gi, gj = m // tm, D_FF // tf
    in_specs = [
        pl.BlockSpec((tm, D_MODEL), lambda i, j: (i, 0)),
        pl.BlockSpec((tm, D_MODEL), lambda i, j: (i, 0)),
        grp["mod_spec"](l, 5, tm),
        pl.BlockSpec((1, D_MODEL, tf), lambda i, j: (0, 0, j)),
        pl.BlockSpec((1, tf, D_MODEL), lambda i, j: (0, j, 0)),
    ]
    args = [h2, x, mod, wu_b, wd_b]
    out_specs = [pl.BlockSpec((tm, D_MODEL), lambda i, j: (i, 0))]
    out_shape = [jax.ShapeDtypeStruct((m, D_MODEL), F32)]
    for w in next_f32:
        rows, width = w.shape[1] // (gi * gj), w.shape[2]
        in_specs.append(pl.BlockSpec((1, rows, width), lambda i, j: (l + 1, i * gj + j, 0)))
        out_specs.append(pl.BlockSpec((1, rows, width), lambda i, j: (0, i * gj + j, 0)))
        out_shape.append(jax.ShapeDtypeStruct((1,) + w.shape[1:], BF16))
        args.append(w)
    return pl.pallas_call(
        functools.partial(_mlp_kernel, n_cast=len(next_f32)),
        grid=(gi, gj),
        in_specs=in_specs,
        out_specs=out_specs,
        out_shape=out_shape,
        scratch_shapes=[pltpu.VMEM((tm, D_MODEL), F32)],
        compiler_params=_cparams(("arbitrary", "arbitrary")),
        name="mlp",
    )(*args)


def _row_iota(shape):
    return lax.broadcasted_iota(jnp.int32, shape, 0)


def _lane_iota(shape):
    return lax.broadcasted_iota(jnp.int32, shape, 1)


def _cat(pieces, axis):
    return pieces[0] if len(pieces) == 1 else jnp.concatenate(pieces, axis=axis)


def _cumsum_rows(x, lseq):
    rows = x.shape[0]
    rr = _row_iota((rows, rows))
    cc = _lane_iota((rows, rows))
    keep = cc <= rr
    if lseq < rows:
        keep = jnp.logical_and(keep, (rr // lseq) == (cc // lseq))
    tri = jnp.where(keep, 1.0, 0.0).astype(BF16)
    hi = x.astype(BF16)
    rem = x - hi.astype(F32)
    mid = rem.astype(BF16)
    lo = (rem - mid.astype(F32)).astype(BF16)
    return (jnp.dot(tri, hi, preferred_element_type=F32) + jnp.dot(tri, mid, preferred_element_type=F32)
            + jnp.dot(tri, lo, preferred_element_type=F32))


def _seq_last_rows(x, nseq, lseq):
    return _cat([jnp.broadcast_to(x[(b + 1) * lseq - 1:(b + 1) * lseq, :], (lseq, x.shape[1]))
                 for b in range(nseq)], 0)


def _hgrn_body(pm_ref, par_ref, st_ref, *, nseq, lseq):
    rows = TILE_ROWS
    dh = HGRN_DH
    aq = pm_ref[:, COL_AQ:COL_AQ + W_GROUP]
    xf = pm_ref[:, COL_AF:COL_AF + W_GROUP]
    v = pm_ref[:, COL_AI:COL_AI + W_GROUP]
    q = aq * _sigmoid(aq)
    e = jnp.exp(-jnp.abs(xf))
    inv = 1.0 / (1.0 + e)
    log_sig = jnp.minimum(xf, 0.0) - jnp.log(1.0 + e)
    log_lb = par_ref[0:1, :]
    bterm = par_ref[1:2, :] + log_sig
    lf = jnp.maximum(log_lb, bterm) + jnp.log(1.0 + jnp.exp(-jnp.abs(log_lb - bterm)))
    one_m_lb = par_ref[2:3, :]
    k = one_m_lb * jnp.where(xf >= 0.0, e * inv, inv)
    f = par_ref[4:5, :] + one_m_lb * jnp.where(xf >= 0.0, inv, e * inv)
    cum = _cumsum_rows(lf, lseq)

    row = _row_iota((rows, W_GROUP))
    heads = [slice(hd * dh, (hd + 1) * dh) for hd in range(HGRN_HEADS)]
    o = [jnp.zeros((rows, dh), F32) for _ in heads]

    sub = min(SUBLANES, lseq)
    fm = jnp.where((row & (sub - 1)) == 0, 0.0, f)
    kg = k
    vs = v
    for d in range(sub):
        if d > 0:
            kg = fm * pltpu.roll(kg, 1, axis=0)
            vs = pltpu.roll(vs, 1, axis=0)
        term = q * kg
        for hd, sl in enumerate(heads):
            o[hd] = o[hd] + jnp.sum(term[:, sl], axis=-1, keepdims=True) * vs[:, sl]

    levels = []
    h = lseq // 2
    while h >= sub:
        levels.append(h)
        h //= 2
    if levels:
        rr = _row_iota((rows, rows))
        cc = _lane_iota((rows, rows))
        p = [jnp.zeros((rows, rows), F32) for _ in heads]
        for h in levels:
            upper = (row & h) != 0
            refm = _cat([jnp.broadcast_to(cum[jb * 2 * h + h - 1:jb * 2 * h + h, :], (2 * h, W_GROUP))
                         for jb in range(rows // (2 * h))], 0)
            x = jnp.exp(jnp.where(upper, cum - refm, refm - cum))
            a_side = jnp.where(upper, q * x, 0.0).astype(BF16)
            b_side = jnp.where(upper, 0.0, k * x).astype(BF16)
            if 2 * h < rows:
                same = (rr // (2 * h)) == (cc // (2 * h))
            for hd, sl in enumerate(heads):
                s = lax.dot_general(a_side[:, sl], b_side[:, sl], _NT, preferred_element_type=F32)
                if 2 * h < rows:
                    s = jnp.where(same, s, 0.0)
                p[hd] = p[hd] + s
        for hd, sl in enumerate(heads):
            o[hd] = o[hd] + _dot(p[hd], v[:, sl])

    qe = (q * jnp.exp(cum)).astype(BF16)
    lastm = _seq_last_rows(cum, nseq, lseq)
    kd = k * jnp.exp(lastm - cum)
    dec = jnp.exp(lastm)
    vb = v.astype(BF16)
    seq_of_row = row // lseq
    for hd, sl in enumerate(heads):
        inter = []
        for b in range(nseq):
            rs = slice(b * lseq, (b + 1) * lseq)
            st = st_ref[b, hd]
            inter.append(lax.dot_general(qe[rs, sl], st.astype(BF16), _NT, preferred_element_type=F32))
            kdb = kd if nseq == 1 else jnp.where(seq_of_row == b, kd, 0.0)
            upd = lax.dot_general(vb[:, sl], kdb[:, sl].astype(BF16), _TN, preferred_element_type=F32)
            st_ref[b, hd] = st * dec[b * lseq:b * lseq + 1, sl] + upd
        o[hd] = o[hd] + _cat(inter, 0)

    gate = _sigmoid(pm_ref[:, COL_AG:COL_AG + W_GROUP])
    out = []
    for hd, sl in enumerate(heads):
        ms = jnp.mean(o[hd] * o[hd], axis=-1, keepdims=True)
        out.append(o[hd] * lax.rsqrt(ms + EPS) * par_ref[3:4, sl] * gate[:, sl])
    return out


def _pool_body(pm_ref, pw_ref, sc_ref, hout_ref, ext, i, *, nseq, lseq, pos0):
    stride = lseq + HIST_PAD
    u = pm_ref[:, COL_PU:COL_PU + W_GROUP]
    for b in range(nseq):
        base = b * stride
        ext[base + HIST_PAD - POOL_BUF:base + HIST_PAD, :] = hout_ref[b]
        ext[base + HIST_PAD:base + HIST_PAD + lseq, :] = u[b * lseq:(b + 1) * lseq, :]

    local = _row_iota((TILE_ROWS, POOL_CH)) & (lseq - 1)
    posn = pos0 + i * lseq + local
    out = []
    for gi, win in enumerate(POOL_WINDOWS):
        cs = slice(gi * POOL_CH, (gi + 1) * POOL_CH)
        pieces = []
        for b in range(nseq):
            base = b * stride + HIST_PAD
            s = ext[base:base + lseq, cs]
            for j in range(1, win):
                s = s + ext[pl.ds(base - j, lseq), cs]
            pieces.append(s)
        cnt = jnp.minimum(posn + 1, win).astype(F32)
        pooled = _cat(pieces, 0) / cnt - u[:, cs]
        out.append(_dot(pooled, pw_ref[gi]) * sc_ref[:, cs])

    for b in range(nseq):
        base = b * stride
        hout_ref[b] = ext[base + lseq + HIST_PAD - POOL_BUF:base + lseq + HIST_PAD, :]
    return out


def _expand_heads(z, emat):
    hi = z.astype(BF16)
    lo = (z - hi.astype(F32)).astype(BF16)
    return (jnp.dot(hi, emat, preferred_element_type=F32) + jnp.dot(lo, emat, preferred_element_type=F32))


def _ssd_body(pm_ref, pt_ref, cw_ref, cb_ref, par_ref, par5_ref, cout_ref, st_ref, ext, *, nseq, lseq):
    rows = TILE_ROWS
    hist = SSM_CONV - 1
    stride = lseq + SUBLANES
    gw = SSM_N

    xbc = pm_ref[:, COL_XBC:COL_XBC + SSM_CONV_DIM]
    for b in range(nseq):
        base = b * stride
        ext[base + SUBLANES - hist:base + SUBLANES, :] = cout_ref[b]
        ext[base + SUBLANES:base + SUBLANES + lseq, :] = xbc[b * lseq:(b + 1) * lseq, :]
    acc = None
    for j in range(SSM_CONV):
        sh = _cat([ext[pl.ds(b * stride + SUBLANES - hist + j, lseq), :] for b in range(nseq)], 0)
        t = sh * cw_ref[j:j + 1, :]
        acc = t if acc is None else acc + t
    for b in range(nseq):
        base = b * stride
        cout_ref[b] = ext[base + lseq + SUBLANES - hist:base + lseq + SUBLANES, :]
    acc = acc + cb_ref[...]
    conv = acc * _sigmoid(acc)
    x = conv[:, 0:W_GROUP]
    bm = conv[:, W_GROUP:W_GROUP + SSM_GROUPS * gw]
    cm = conv[:, W_GROUP + SSM_GROUPS * gw:]

    pre = pt_ref[:, COL_DT:COL_DT + LANES] + par_ref[0:1, :]
    dt = jnp.maximum(pre, 0.0) + jnp.log1p(jnp.exp(-jnp.abs(pre)))
    a = dt * (-jnp.exp(par_ref[1:2, :]))
    cum = _cumsum_rows(a, lseq)
    cum_t = cum.T
    dt_t = dt.T

    rr = _row_iota((rows, rows))
    cc = _lane_iota((rows, rows))
    valid = cc <= rr
    if nseq > 1:
        valid = jnp.logical_and(valid, (rr // lseq) == (cc // lseq))

    erow = _row_iota((LANES, W_GROUP))
    ecol = _lane_iota((LANES, W_GROUP))
    emat = jnp.where(ecol // SSM_P == erow, 1.0, 0.0).astype(BF16)

    lane = _lane_iota((rows, LANES))
    lo_half = lane < SSM_P
    heads_per_group = SSM_HEADS // SSM_GROUPS
    y_chunks = []
    for g in range(SSM_GROUPS):
        gs = slice(g * gw, (g + 1) * gw)
        cb = lax.dot_general(cm[:, gs].astype(BF16), bm[:, gs].astype(BF16), _NT, preferred_element_type=F32)
        for jc in range(heads_per_group // 2):
            chunk = g * (heads_per_group // 2) + jc
            xc = x[:, chunk * LANES:(chunk + 1) * LANES]
            yc = None
            for half in range(2):
                r = 2 * chunk + half
                seg = jnp.broadcast_to(cum[:, r:r + 1], (rows, rows)) - cum_t[r:r + 1, :]
                wts = cb * jnp.exp(jnp.where(valid, seg, NEG_BIG)) * dt_t[r:r + 1, :]
                xm = jnp.where(lo_half, xc, 0.0) if half == 0 else jnp.where(lo_half, 0.0, xc)
                t = _dot(wts, xm)
                yc = t if yc is None else yc + t
            y_chunks.append(yc)
    y = jnp.concatenate(y_chunks, axis=1)

    lastm = _seq_last_rows(cum, nseq, lseq)
    e_cum = _expand_heads(jnp.exp(cum), emat)
    e_wst = _expand_heads(dt * jnp.exp(lastm - cum), emat)
    e_dec = _expand_heads(jnp.exp(lastm), emat)
    xs = x * e_wst
    row5 = _row_iota((rows, W_GROUP))
    inter_groups = []
    pw = heads_per_group * SSM_P
    for g in range(SSM_GROUPS):
        gs = slice(g * gw, (g + 1) * gw)
        ps = slice(g * pw, (g + 1) * pw)
        inter = []
        for b in range(nseq):
            rs = slice(b * lseq, (b + 1) * lseq)
            st = st_ref[b, g]
            inter.append(_dot(cm[rs, gs], st))
            xsb = xs if nseq == 1 else jnp.where(row5 // lseq == b, xs, 0.0)
            upd = _dot_tn(bm[:, gs], xsb[:, ps])
            st_ref[b, g] = st * e_dec[b * lseq:b * lseq + 1, ps] + upd
        inter_groups.append(_cat(inter, 0))
    y = y + jnp.concatenate(inter_groups, axis=1) * e_cum
    y = y + x * par5_ref[0:1, :]
    z = pm_ref[:, COL_CZ:COL_CZ + W_GROUP]
    y = y * (z * _sigmoid(z))
    ms = jnp.mean(y * y, axis=-1, keepdims=True)
    return y * lax.rsqrt(ms + EPS) * par5_ref[1:2, :]


def _head_rmsnorm(x, g, lo_half):
    sq = x * x
    s_lo = jnp.sum(jnp.where(lo_half, sq, 0.0), axis=-1, keepdims=True)
    s_hi = jnp.sum(jnp.where(lo_half, 0.0, sq), axis=-1, keepdims=True)
    ms = jnp.where(lo_half, s_lo, s_hi) * (1.0 / ATTN_DH)
    return x * lax.rsqrt(ms + EPS) * g


def _rope(x, cos, sin_signed, upper_half):
    partner = jnp.where(upper_half, pltpu.roll(x, ATTN_DH // 2, axis=1), pltpu.roll(x, LANES - ATTN_DH // 2, axis=1))
    return x * cos + partner * sin_signed


def _dup_head(x, g, lane):
    own = jnp.where((lane // ATTN_DH) == g, x, 0.0)
    return own + pltpu.roll(own, ATTN_DH, axis=1)


def _swa_body(pt_ref, cos_ref, sin_ref, par_ref, sink_ref, kout_ref, vout_ref, i, *, nseq, lseq, has_state):
    rows = TILE_ROWS
    rep = ATTN_HEADS // ATTN_KV
    lane = _lane_iota((rows, LANES))
    lo_half = lane < ATTN_DH
    upper_half = (lane & (ATTN_DH // 2)) != 0
    cos = cos_ref[...]
    sin = sin_ref[...]
    kn = _rope(_head_rmsnorm(pt_ref[:, COL_DK:COL_DK + LANES], par_ref[1:2, :], lo_half), cos, sin, upper_half)
    vn = pt_ref[:, COL_DV:COL_DV + LANES]
    scale = ATTN_DH ** -0.5
    qs = []
    for c in range(ATTN_HEADS // 2):
        qc = pt_ref[:, COL_DQ + c * LANES:COL_DQ + (c + 1) * LANES]
        qs.append(_rope(_head_rmsnorm(qc, par_ref[0:1, :], lo_half), cos, sin, upper_half) * scale)

    mq = rep * lseq
    qrow = _row_iota((mq, LANES))
    q_local = qrow & (lseq - 1)
    kcol = _lane_iota((mq, LANES))
    hist_ok = jnp.logical_or(has_state, i > 0)
    hist_valid = jnp.logical_and(kcol > q_local, hist_ok)
    lane_w = _lane_iota((WINDOW, LANES))

    out_rows = [[None] * nseq for _ in range(ATTN_HEADS // 2)]
    for g in range(ATTN_KV):
        k_new = _dup_head(kn, g, lane).astype(BF16)
        v_new = _dup_head(vn, g, lane).astype(BF16)
        for b in range(nseq):
            rs = slice(b * lseq, (b + 1) * lseq)
            q4 = []
            sink_rows = []
            for r in range(rep):
                hidx = g * rep + r
                qc = qs[hidx // 2][rs, :]
                lo_l = lo_half[0:lseq, :]
                q4.append(jnp.where(lo_l, qc, 0.0) if hidx % 2 == 0 else jnp.where(lo_l, 0.0, qc))
                sink_rows.append(jnp.broadcast_to(sink_ref[hidx:hidx + 1, 0:1], (lseq, 1)))
            q4 = jnp.concatenate(q4, axis=0).astype(BF16)
            sink = jnp.concatenate(sink_rows, axis=0)
            k_hist = _dup_head(kout_ref[b], g, lane_w).astype(BF16)
            v_hist = _dup_head(vout_ref[b], g, lane_w).astype(BF16)
            s_h = lax.dot_general(q4, k_hist, _NT, preferred_element_type=F32)
            s_n = lax.dot_general(q4, k_new, _NT, preferred_element_type=F32)
            s_h = jnp.where(hist_valid, s_h, NEG_BIG)
            new_valid = jnp.logical_and(kcol // lseq == b, (kcol & (lseq - 1)) <= q_local)
            s_n = jnp.where(new_valid, s_n, NEG_BIG)
            m = jnp.maximum(jnp.maximum(jnp.max(s_h, axis=-1, keepdims=True),
                                        jnp.max(s_n, axis=-1, keepdims=True)), sink)
            p_h = jnp.exp(s_h - m)
            p_n = jnp.exp(s_n - m)
            den = (jnp.sum(p_h, axis=-1, keepdims=True) + jnp.sum(p_n, axis=-1, keepdims=True)
                   + jnp.exp(sink - m))
            o4 = (jnp.dot(p_h.astype(BF16), v_hist, preferred_element_type=F32)
                  + jnp.dot(p_n.astype(BF16), v_new, preferred_element_type=F32)) / den
            for pair in range(rep // 2):
                c = (g * rep) // 2 + pair
                o_lo = o4[(2 * pair) * lseq:(2 * pair + 1) * lseq, :]
                o_hi = o4[(2 * pair + 1) * lseq:(2 * pair + 2) * lseq, :]
                out_rows[c][b] = jnp.where(lo_half[0:lseq, :], o_lo, o_hi)

    for b in range(nseq):
        rs = slice(b * lseq, (b + 1) * lseq)
        if lseq < WINDOW:
            keep_k = kout_ref[b, lseq:WINDOW, :]
            keep_v = vout_ref[b, lseq:WINDOW, :]
            kout_ref[b, 0:WINDOW - lseq, :] = keep_k
            vout_ref[b, 0:WINDOW - lseq, :] = keep_v
        kout_ref[b, WINDOW - lseq:WINDOW, :] = kn[rs, :]
        vout_ref[b, WINDOW - lseq:WINDOW, :] = vn[rs, :]
    return [_cat(out_rows[c], 0) for c in range(ATTN_HEADS // 2)]


N_STATES = 6


def _transpose_heads(st_ref, n):
    for b in range(n):
        for hd in range(HGRN_HEADS):
            st_ref[b, hd] = st_ref[b, hd].T


def _mix_kernel(*refs, nseq, lseq, has_state, pos0):
    (pm_ref, pt_ref, x_ref, gate_ref, g2_ref, sc2_ref, sh2_ref, w_ref, cos_ref, sin_ref, hpar_ref, pw_ref, psc_ref,
     cw_ref, cb_ref, spar_ref, spar5_ref, apar_ref, sink_ref) = refs[:19]
    rest = refs[19:]
    if has_state:
        init_refs, rest = rest[:N_STATES], rest[N_STATES:]
    else:
        init_refs = (None,) * N_STATES
    o_ref, h2_ref = rest[:2]
    state_refs = rest[2:2 + N_STATES]
    pool_ext, conv_ext = rest[2 + N_STATES:]
    nh_ref, np_ref, nc_ref, ns_ref, nk_ref, nv_ref = state_refs
    i = pl.program_id(1)

    @pl.when(i == 0)
    def _():
        for dst, src in zip(state_refs, init_refs):
            if src is None:
                dst[...] = jnp.zeros_like(dst)
            elif dst is nh_ref:
                for b in range(nseq):
                    for hd in range(HGRN_HEADS):
                        dst[b, hd] = src[b, hd].T
            else:
                dst[...] = src[...]

    ya = _hgrn_body(pm_ref, hpar_ref, nh_ref, nseq=nseq, lseq=lseq)
    yb = _pool_body(pm_ref, pw_ref, psc_ref, np_ref, pool_ext, i, nseq=nseq, lseq=lseq, pos0=pos0)
    yc = _ssd_body(pm_ref, pt_ref, cw_ref, cb_ref, spar_ref, spar5_ref, nc_ref, ns_ref, conv_ext, nseq=nseq, lseq=lseq)
    yd = _swa_body(pt_ref, cos_ref, sin_ref, apar_ref, sink_ref, nk_ref, nv_ref, i, nseq=nseq, lseq=lseq,
                   has_state=has_state)
    y = jnp.concatenate([t.astype(BF16) for t in ya + yb + [yc] + yd], axis=1)
    mixed = jnp.dot(y, w_ref[...], preferred_element_type=F32)
    x1 = x_ref[...] + gate_ref[...] * mixed
    o_ref[...] = x1
    h2_ref[...] = _modulated_norm(x1, g2_ref[...], sc2_ref[...], sh2_ref[...]).astype(BF16)

    @pl.when(i == pl.num_programs(1) - 1)
    def _():
        _transpose_heads(nh_ref, nseq)


def _mix_call(grp, l, x, proj, tail, g2, mod, w_out, cos, sin, pars, states):
    nseq, lseq, no, nt = grp["nseq"], grp["lseq"], grp["no"], grp["nt"]
    has_state = states is not None
    hpar, pool_w, pool_sc, conv_w, conv_b, spar, spar5, apar, sink = pars
    tails = [(HGRN_HEADS, HGRN_DH, HGRN_DH), (POOL_BUF, W_GROUP), (SSM_CONV - 1, SSM_CONV_DIM),
             (SSM_GROUPS, SSM_N, (SSM_HEADS // SSM_GROUPS) * SSM_P),
             (WINDOW, ATTN_KV * ATTN_DH), (WINDOW, ATTN_KV * ATTN_DH)]

    def row_spec(width):
        return pl.BlockSpec((TILE_ROWS, width), lambda o, i: (o * nt + i, 0))

    in_specs = [
        row_spec(MAIN_WIDTH), row_spec(TAIL_WIDTH), row_spec(D_MODEL),
        grp["mod_spec"](l, 2, TILE_ROWS, lambda o, i: o * nt + i),
        _layer_spec(l, (1, D_MODEL)),
        grp["mod_spec"](l, 4, TILE_ROWS, lambda o, i: o * nt + i),
        grp["mod_spec"](l, 3, TILE_ROWS, lambda o, i: o * nt + i),
        _layer_spec(l, (D_MODEL, D_MODEL), pipeline_mode=pl.Buffered(1)),
        pl.BlockSpec((TILE_ROWS, LANES), lambda o, i: (i, 0)),
        pl.BlockSpec((TILE_ROWS, LANES), lambda o, i: (i, 0)),
        _layer_spec(l, (8, W_GROUP)), _layer_spec(l, (len(POOL_WINDOWS), POOL_CH, POOL_CH)),
        _layer_spec(l, (1, W_GROUP)), _layer_spec(l, (SSM_CONV, SSM_CONV_DIM)), _layer_spec(l, (1, SSM_CONV_DIM)),
        _layer_spec(l, (8, LANES)), _layer_spec(l, (8, W_GROUP)), _layer_spec(l, (8, LANES)), _layer_spec(l, (8, LANES)),
    ]
    args = [proj, tail, x, mod, g2, mod, mod, w_out, cos, sin, hpar, pool_w, pool_sc, conv_w, conv_b, spar, spar5,
            apar, sink]
    if has_state:
        for t in tails:
            zeros = (0,) * len(t)
            in_specs.append(pl.BlockSpec((None, nseq) + t, lambda o, i, zeros=zeros: (l, o) + zeros,
                                         pipeline_mode=pl.Buffered(1)))
        args += list(states)
    out_specs = [row_spec(D_MODEL), row_spec(D_MODEL)]
    out_shape = [jax.ShapeDtypeStruct((grp["m"], D_MODEL), F32), jax.ShapeDtypeStruct((grp["m"], D_MODEL), BF16)]
    state_mode = dict(pipeline_mode=pl.Buffered(1)) if nseq > 1 else {}
    for t in tails:
        zeros = (0,) * len(t)
        out_specs.append(pl.BlockSpec((nseq,) + t, lambda o, i, zeros=zeros: (o,) + zeros, **state_mode))
        out_shape.append(jax.ShapeDtypeStruct((grp["nb"],) + t, F32))
    return pl.pallas_call(
        functools.partial(_mix_kernel, nseq=nseq, lseq=lseq, has_state=has_state, pos0=grp["pos0"]),
        grid=(no, nt),
        in_specs=in_specs,
        out_specs=out_specs,
        out_shape=out_shape,
        scratch_shapes=[pltpu.VMEM((nseq * (lseq + HIST_PAD), W_GROUP), F32),
                        pltpu.VMEM((nseq * (lseq + SUBLANES), SSM_CONV_DIM), F32)],
        compiler_params=_cparams(("arbitrary", "arbitrary")),
        name="mix",
    )(*args)


def _front_kernel(*refs, nsub, pos0):
    (x_ref, g_ref, g2_ref, mod_ref, wm_ref, wt_ref, w_ref, cos_ref, sin_ref, hpar_ref, pw_ref, psc_ref, cw_ref,
     cb_ref, spar_ref, spar5_ref, apar_ref, sink_ref) = refs[:18]
    o_ref, h2_ref = refs[18:20]
    state_refs = refs[20:20 + N_STATES]
    pm_scr, pt_scr = refs[20 + N_STATES:22 + N_STATES]
    ext_refs = refs[22 + N_STATES:]
    rows = TILE_ROWS
    i = pl.program_id(0)

    @pl.when(i == 0)
    def _():
        for dst in state_refs:
            dst[...] = jnp.zeros_like(dst)

    h = _cat([_modulated_norm(x_ref[s], g_ref[...], mod_ref[s * N_MOD + 1], mod_ref[s * N_MOD + 0]).astype(BF16)
              for s in range(nsub)], 0)
    pm_scr[...] = jnp.dot(h, wm_ref[...], preferred_element_type=F32)
    pt_scr[...] = jnp.dot(h, wt_ref[...], preferred_element_type=F32)

    ys = []
    for s in range(nsub):
        pm = pm_scr.at[pl.ds(s * rows, rows)]
        pt = pt_scr.at[pl.ds(s * rows, rows)]
        nh, npool, nc, ns, nk, nv = (r.at[pl.ds(s, 1)] for r in state_refs)
        ya = _hgrn_body(pm, hpar_ref, nh, nseq=1, lseq=rows)
        yb = _pool_body(pm, pw_ref, psc_ref, npool, ext_refs[2 * s], i, nseq=1, lseq=rows, pos0=pos0)
        yc = _ssd_body(pm, pt, cw_ref, cb_ref, spar_ref, spar5_ref, nc, ns, ext_refs[2 * s + 1], nseq=1, lseq=rows)
        yd = _swa_body(pt, cos_ref, sin_ref, apar_ref, sink_ref, nk, nv, i, nseq=1, lseq=rows, has_state=False)
        ys.append(jnp.concatenate([t.astype(BF16) for t in ya + yb + [yc] + yd], axis=1))
    mixed = jnp.dot(_cat(ys, 0), w_ref[...], preferred_element_type=F32)
    for s in range(nsub):
        x1 = x_ref[s] + mod_ref[s * N_MOD + 2] * mixed[s * rows:(s + 1) * rows, :]
        o_ref[s] = x1
        h2_ref[s] = _modulated_norm(x1, g2_ref[...], mod_ref[s * N_MOD + 4], mod_ref[s * N_MOD + 3]).astype(BF16)

    @pl.when(i == pl.num_programs(0) - 1)
    def _():
        _transpose_heads(state_refs[0], nsub)


def _front_call(grp, l, x, g, g2, mod, w_in_b, w_tail, w_out, cos, sin, pars):
    nb, nt = grp["nb"], grp["nt"]
    hpar, pool_w, pool_sc, conv_w, conv_b, spar, spar5, apar, sink = pars
    tails = [(HGRN_HEADS, HGRN_DH, HGRN_DH), (POOL_BUF, W_GROUP), (SSM_CONV - 1, SSM_CONV_DIM),
             (SSM_GROUPS, SSM_N, (SSM_HEADS // SSM_GROUPS) * SSM_P),
             (WINDOW, ATTN_KV * ATTN_DH), (WINDOW, ATTN_KV * ATTN_DH)]
    once = dict(pipeline_mode=pl.Buffered(1))
    x_spec = pl.BlockSpec((nb, TILE_ROWS, D_MODEL), lambda i: (0, i, 0))
    in_specs = [
        x_spec, _layer_spec(l, (1, D_MODEL)), _layer_spec(l, (1, D_MODEL)), _layer_spec(l, (nb * N_MOD, 1, D_MODEL)),
        _layer_spec(l, (D_MODEL, MAIN_WIDTH), **once), _layer_spec(l, (D_MODEL, TAIL_WIDTH), **once),
        _layer_spec(l, (D_MODEL, D_MODEL), **once),
        pl.BlockSpec((TILE_ROWS, LANES), lambda i: (i, 0)), pl.BlockSpec((TILE_ROWS, LANES), lambda i: (i, 0)),
        _layer_spec(l, (8, W_GROUP)), _layer_spec(l, (len(POOL_WINDOWS), POOL_CH, POOL_CH)),
        _layer_spec(l, (1, W_GROUP)), _layer_spec(l, (SSM_CONV, SSM_CONV_DIM)), _layer_spec(l, (1, SSM_CONV_DIM)),
        _layer_spec(l, (8, LANES)), _layer_spec(l, (8, W_GROUP)), _layer_spec(l, (8, LANES)), _layer_spec(l, (8, LANES)),
    ]
    args = [x, g, g2, mod, w_in_b, w_tail, w_out, cos, sin, hpar, pool_w, pool_sc, conv_w, conv_b, spar, spar5, apar, sink]
    out_specs = [x_spec, x_spec]
    out_shape = [jax.ShapeDtypeStruct(x.shape, F32), jax.ShapeDtypeStruct(x.shape, BF16)]
    for t in tails:
        zeros = (0,) * (len(t) + 1)
        out_specs.append(pl.BlockSpec((nb,) + t, lambda i, zeros=zeros: zeros))
        out_shape.append(jax.ShapeDtypeStruct((nb,) + t, F32))
    scratch = [pltpu.VMEM((nb * TILE_ROWS, MAIN_WIDTH), F32), pltpu.VMEM((nb * TILE_ROWS, TAIL_WIDTH), F32)]
    for _ in range(nb):
        scratch += [pltpu.VMEM((TILE_ROWS + HIST_PAD, W_GROUP), F32),
                    pltpu.VMEM((TILE_ROWS + SUBLANES, SSM_CONV_DIM), F32)]
    return pl.pallas_call(
        functools.partial(_front_kernel, nsub=nb, pos0=grp["pos0"]),
        grid=(nt,),
        in_specs=in_specs,
        out_specs=out_specs,
        out_shape=out_shape,
        scratch_shapes=scratch,
        compiler_params=_cparams(("arbitrary",)),
        name="front",
    )(*args)


def _rope_tables(pos):
    half = ATTN_DH // 2
    inv = ROPE_THETA ** (-jnp.arange(half, dtype=F32) / half)
    ang = pos.astype(F32)[:, None] * inv[None]
    cos = jnp.tile(jnp.cos(ang), (1, LANES // half))
    sin = jnp.sin(ang)
    sin_signed = jnp.tile(jnp.concatenate([-sin, sin], axis=1), (1, LANES // ATTN_DH))
    return cos, sin_signed


def _pad_lanes(v, width):
    return jnp.pad(v, (0, width - v.shape[0]))


def _rows8(rows_list, width):
    out = jnp.zeros((8, width), F32)
    for r, v in enumerate(rows_list):
        out = out.at[r].set(v)
    return out


def _make_group(nb, seq_len, pos0, mod_rows, tm):
    m = nb * seq_len
    if seq_len >= TILE_ROWS:
        nseq, lseq = 1, TILE_ROWS
        no, nt = nb, seq_len // TILE_ROWS
    else:
        nseq, lseq = TILE_ROWS // seq_len, seq_len
        no, nt = m // TILE_ROWS, 1
    per_row = mod_rows.shape[-2] != 1

    def mod_spec(l, kind, tile, row_tile=lambda i, *_: i):
        if per_row:
            def imap(*idx):
                return (l, kind, row_tile(*idx), 0)
            return pl.BlockSpec((None, None, tile, D_MODEL), imap)

        def imap(*idx):
            return (l, (row_tile(*idx) * tile) // seq_len * N_MOD + kind, 0, 0)
        return pl.BlockSpec((None, None, 1, D_MODEL), imap)

    return dict(nb=nb, seq=seq_len, m=m, tm=tm, tm_mlp=min(m, MLP_ROWS), nseq=nseq, lseq=lseq, no=no, nt=nt,
                pos0=pos0, mod_spec=mod_spec)


def _trunk(grp, x, mod, states, cos, sin, wts, bf16_w):
    w_in_b, w_tail, w_out_b, f32_w, norm1_g, norm2_g, mix_pars = wts
    outs = [[] for _ in range(N_STATES)]
    for l in range(DEPTH):
        wu_b, wd_b = bf16_w[l]
        if grp["nseq"] == 1 and states is None:
            x3 = x.reshape(grp["nb"], grp["seq"], D_MODEL)
            x3, h2, *new_states = _front_call(grp, l, x3, norm1_g, norm2_g, mod, w_in_b, w_tail, w_out_b, cos, sin,
                                              mix_pars)
            x, h2 = x3.reshape(grp["m"], D_MODEL), h2.reshape(grp["m"], D_MODEL)
        else:
            proj, tail = _inproj_call(grp, l, x, norm1_g, mod, w_in_b, w_tail)
            x, h2, *new_states = _mix_call(grp, l, x, proj, tail, norm2_g, mod, w_out_b, cos, sin, mix_pars, states)
        if l + 1 < DEPTH and bf16_w[l + 1] is None:
            x, *converted = _mlp_call(grp, l, h2, x, mod, wu_b, wd_b, f32_w)
            bf16_w[l + 1] = converted
        else:
            x, = _mlp_call(grp, l, h2, x, mod, wu_b, wd_b)
        for lst, val in zip(outs, new_states):
            lst.append(val)
    return x, [jnp.stack(o) for o in outs]


def _ssm_state_to_kernel(s):
    lead = s.shape[:-3]
    r = SSM_HEADS // SSM_GROUPS
    s = s.reshape(lead + (SSM_GROUPS, r, SSM_P, SSM_N))
    s = jnp.moveaxis(s, -1, -3)
    return s.reshape(lead + (SSM_GROUPS, SSM_N, r * SSM_P))


def _ssm_state_from_kernel(s):
    lead = s.shape[:-3]
    r = SSM_HEADS // SSM_GROUPS
    s = s.reshape(lead + (SSM_GROUPS, SSM_N, r, SSM_P))
    s = jnp.moveaxis(s, -3, -1)
    return s.reshape(lead + (SSM_HEADS, SSM_P, SSM_N))


def kernel(x_prompt, x_sample, c_prompt, c_sample, state_hgrn, state_pool, state_ssm, state_conv, cache_k, cache_v, norm1_g, norm2_g, w_ada, b_ada, w_in, hgrn_lb_logits, hgrn_norm_g, pool_w, pool_scale, conv_w, conv_b, dt_bias, a_log, d_skip, ssm_norm_g, q_norm_g, k_norm_g, sinks, w_out, w_up, w_down):
    bp, seq, _ = x_prompt.shape
    bs, dseq, _ = x_sample.shape

    c_all = jnp.concatenate([c_prompt, c_sample], axis=0)
    mod_all, w_in_b, w_out_b = _ada_call(c_all, w_ada, b_ada, w_in, w_out)

    dt_end = MAIN_WIDTH + SSM_HEADS
    w_tail = jnp.concatenate(
        [w_in_b[:, :, dt_end:], w_in_b[:, :, MAIN_WIDTH:dt_end],
         jnp.zeros((DEPTH, D_MODEL, LANES - SSM_HEADS), BF16)], axis=-1)
    f32_w = (w_up, w_down)
    bf16_w = [[w[0:1].astype(BF16) for w in f32_w]] + [None] * (DEPTH - 1)
    pool_w_b = pool_w.astype(BF16)

    p = jax.nn.softmax(hgrn_lb_logits.astype(F32), axis=0)
    cs = jnp.cumsum(p, axis=0)
    lbs = cs - cs[:1]
    hgrn_par = jnp.stack([_rows8([jnp.log(lbs[l]), jnp.log1p(-lbs[l]), 1.0 - lbs[l], hgrn_norm_g[l], lbs[l]], W_GROUP)
                          for l in range(DEPTH)])
    ssd_par = jnp.stack([_rows8([_pad_lanes(dt_bias[l], LANES), _pad_lanes(a_log[l], LANES)], LANES)
                         for l in range(DEPTH)])
    ssd_par5 = jnp.stack([_rows8([jnp.repeat(d_skip[l], SSM_P), ssm_norm_g[l]], W_GROUP) for l in range(DEPTH)])
    swa_par = jnp.stack([_rows8([jnp.tile(q_norm_g[l], 2), jnp.tile(k_norm_g[l], 2)], LANES) for l in range(DEPTH)])
    sink_par = jnp.broadcast_to(sinks[:, :, None], (DEPTH, ATTN_HEADS, LANES))
    mix_pars = (hgrn_par, pool_w_b, pool_scale.reshape(DEPTH, 1, W_GROUP), conv_w,
                conv_b.reshape(DEPTH, 1, SSM_CONV_DIM), ssd_par, ssd_par5, swa_par, sink_par)
    wts = (w_in_b, w_tail, w_out_b, f32_w, norm1_g.reshape(DEPTH, 1, D_MODEL),
           norm2_g.reshape(DEPTH, 1, D_MODEL), mix_pars)

    mod_p =mod_all[:, :bp].reshape(DEPTH, bp * N_MOD, 1, D_MODEL)
    mod_s = mod_all[:, bp:].reshape(DEPTH, bs, N_MOD, D_MODEL)
    mod_s = jnp.repeat(jnp.moveaxis(mod_s, 2, 1), dseq, axis=2)

    grp_p = _make_group(bp, seq, 0, mod_p, 1024)
    grp_s = _make_group(bs, dseq, PAST_LEN, mod_s, bs * dseq)

    cos_p, sin_p = _rope_tables(jnp.arange(seq))
    cos_s, sin_s = _rope_tables(PAST_LEN + (jnp.arange(TILE_ROWS) % dseq))

    y_p, st_p = _trunk(grp_p, x_prompt.reshape(bp * seq, D_MODEL), mod_p, None, cos_p, sin_p, wts, bf16_w)

    kv_flat = (DEPTH, bs, WINDOW, ATTN_KV * ATTN_DH)
    states = (state_hgrn, state_pool, state_conv, _ssm_state_to_kernel(state_ssm),
              cache_k.reshape(kv_flat), cache_v.reshape(kv_flat))
    y_s, st_s = _trunk(grp_s, x_sample.reshape(bs * dseq, D_MODEL), mod_s, states, cos_s, sin_s, wts, bf16_w)

    def finish(st, nb):
        n_h, n_p, n_c, n_s, n_k, n_v = st
        return (n_h, n_p, _ssm_state_from_kernel(n_s), n_c,
                n_k.reshape(DEPTH, nb, WINDOW, ATTN_KV, ATTN_DH), n_v.reshape(DEPTH, nb, WINDOW, ATTN_KV, ATTN_DH))

    return ((y_p.reshape(bp, seq, D_MODEL), y_s.reshape(bs, dseq, D_MODEL)) + finish(st_p, bp) + finish(st_s, bs))
```
